```python
import math, functools
import jax, jax.numpy as jnp
from jax import lax
import numpy as np


D_MODEL = 1024
BATCH = 8
SEQ = 2048
DEPTH = 2
DEC_BATCH = 128
DEC_SEQ = 4
PAST_LEN = 16384
PAGE_SIZE = 128

N_META = 16
N_EVEN = (DEPTH + 1) // 2
N_ODD = DEPTH // 2
H_A = 4
DK_A = 128
DV_A = 128
H_B = 4
DK_B = 64
DV_B = 128
GLA_RANK = 16
GLA_TAU = 16.0
LA_CHUNK = 32
D_INNER = 2 * D_MODEL
HEAD_P = 64
H_C = D_INNER // HEAD_P
D_STATE = 128
N_GROUPS = 4
D_CONV = 4
CONV_DIM = D_INNER + 2 * N_GROUPS * D_STATE
SSD_CHUNK = 64
D_FF = -(-8 * D_MODEL // (3 * 256)) * 256
EPS = 1e-6

IN_EVEN = 2 * H_A * DK_A + 2 * H_A * DV_A + 2 * H_B * DK_B + 2 * H_B * DV_B + GLA_RANK
OUT_EVEN = H_A * DV_A + H_B * DV_B
IN_ODD = D_INNER + CONV_DIM + H_C

kernel_name = 'hybrid_hgrn2_gla_mamba2_step'


def rmsnorm(x, w):
    xf = x.astype(jnp.float32)
    y = xf * lax.rsqrt(jnp.mean(xf * xf, axis=-1, keepdims=True) + EPS)
    return (y * w.astype(jnp.float32)).astype(x.dtype)


def run_segments(scan_fn, arrays, state, seg_lens):
    outs = []
    start = 0
    for length in seg_lens:
        o, state = scan_fn(*[a[:, start:start + length] for a in arrays], state)
        outs.append(o)
        start += length
    return jnp.concatenate(outs, axis=1), state


def gated_linear_scan(q, k, v, log_f, s0):
    f32 = jnp.float32
    b, t, h, dk = q.shape
    dv = v.shape[-1]
    c = math.gcd(t, LA_CHUNK)
    n = t // c
    qc = q.astype(f32).reshape(b, n, c, h, dk)
    kc = k.astype(f32).reshape(b, n, c, h, dk)
    vc = v.astype(f32).reshape(b, n, c, h, dv)
    cum = jnp.cumsum(log_f.astype(f32).reshape(b, n, c, h, dk), axis=2)
    last = cum[:, :, -1:]
    q_e = qc * jnp.exp(cum)
    k_e = kc * jnp.exp(-cum)
    k_end = kc * jnp.exp(last - cum)
    causal = jnp.tril(jnp.ones((c, c), dtype=bool))
    scores = jnp.where(causal, jnp.einsum('bnihk,bnjhk->bnhij', q_e, k_e), 0.0)
    o_intra = jnp.einsum('bnhij,bnjhv->bnihv', scores, vc)
    u = jnp.einsum('bnjhk,bnjhv->bnhkv', k_end, vc)
    decay = jnp.exp(last[:, :, 0])

    def step(s, inp):
        d_n, u_n = inp
        return d_n[..., None] * s + u_n, s

    s_last, s_start = lax.scan(step, s0.astype(f32), (jnp.moveaxis(decay, 1, 0), jnp.moveaxis(u, 1, 0)))
    s_start = jnp.moveaxis(s_start, 0, 1)
    o_inter = jnp.einsum('bnihk,bnhkv->bnihv', q_e, s_start)
    return (o_intra + o_inter).reshape(b, t, h, dv), s_last


def ssd_scan(x, dt, bm, cm, s0, a_neg):
    f32 = jnp.float32
    b, t, h, p = x.shape
    g, ds = bm.shape[2], bm.shape[3]
    r = h // g
    c = math.gcd(t, SSD_CHUNK)
    n = t // c
    xc = x.astype(f32).reshape(b, n, c, g, r, p)
    dtc = dt.astype(f32).reshape(b, n, c, g, r)
    bc = bm.astype(f32).reshape(b, n, c, g, ds)
    cc = cm.astype(f32).reshape(b, n, c, g, ds)
    cum = jnp.cumsum(dtc * a_neg.reshape(g, r), axis=2)
    cum_t = jnp.moveaxis(cum, 2, -1)
    diff = cum_t[..., :, None] - cum_t[..., None, :]
    causal = jnp.tril(jnp.ones((c, c), dtype=bool))
    decay_ij = jnp.exp(jnp.where(causal, diff, -jnp.inf))
    xdt = xc * dtc[..., None]
    cb = jnp.einsum('bnigs,bnjgs->bngij', cc, bc)
    y_intra = jnp.einsum('bngrij,bnjgrp->bnigrp', cb[:, :, :, None] * decay_ij, xdt)
    decay_end = jnp.exp(cum[:, :, -1:] - cum)
    u = jnp.einsum('bnjgrp,bnjgs->bngrps', xdt * decay_end[..., None], bc)
    chunk_decay = jnp.exp(cum[:, :, -1])

    def step(s, inp):
        d_n, u_n = inp
        return d_n[..., None, None] * s + u_n, s

    s_init = s0.astype(f32).reshape(b, g, r, p, ds)
    s_last, s_start = lax.scan(step, s_init, (jnp.moveaxis(chunk_decay, 1, 0), jnp.moveaxis(u, 1, 0)))
    s_start = jnp.moveaxis(s_start, 0, 1)
    y_inter = jnp.einsum('bnigs,bngrps->bnigrp', cc, s_start) * jnp.exp(cum)[..., None]
    return (y_intra + y_inter).reshape(b, t, h, p), s_last.reshape(b, h, p, ds)


def even_mixer(h, s_hgrn, s_gla, seg, lb, w_in, w_alpha_up, b_alpha, norm_a, norm_b, w_out):
    f32 = jnp.float32
    b, t, _ = h.shape
    wa_k, wa_v, wb_k, wb_v = H_A * DK_A, H_A * DV_A, H_B * DK_B, H_B * DV_B
    sizes = [wa_k, wa_k, wa_v, wa_v, wb_k, wb_k, wb_v, wb_v, GLA_RANK]
    idx = [int(v) for v in np.cumsum(sizes)[:-1]]
    qa, fa, ia, ga, qb, kb, vb, gb, alow = jnp.split(h @ w_in, idx, axis=-1)
    lbh = lb.reshape(H_A, DK_A)
    f_a = lbh + (1.0 - lbh) * jax.nn.sigmoid(fa.astype(f32).reshape(b, t, H_A, DK_A))
    o_a, s_a = run_segments(gated_linear_scan,
                            [qa.reshape(b, t, H_A, DK_A), 1.0 - f_a, ia.reshape(b, t, H_A, DV_A), jnp.log(f_a)],
                            s_hgrn, seg)
    o_a = rmsnorm(o_a, norm_a) * jax.nn.silu(ga.astype(f32).reshape(b, t, H_A, DV_A))
    log_alpha = jax.nn.log_sigmoid((alow @ w_alpha_up + b_alpha).astype(f32)) / GLA_TAU
    o_b, s_b = run_segments(gated_linear_scan,
                            [qb.reshape(b, t, H_B, DK_B) * (DK_B ** -0.5), kb.reshape(b, t, H_B, DK_B),
                             vb.reshape(b, t, H_B, DV_B), log_alpha.reshape(b, t, H_B, DK_B)],
                            s_gla, seg)
    o_b = rmsnorm(o_b, norm_b) * jax.nn.silu(gb.astype(f32).reshape(b, t, H_B, DV_B))
    o = jnp.concatenate([o_a.reshape(b, t, wa_v), o_b.reshape(b, t, wb_v)], axis=-1).astype(h.dtype)
    return o @ w_out, s_a, s_b


def odd_mixer(h, s_ssm, s_conv, seg, w_in, conv_w, conv_b, dt_bias, a_log, d_skip, norm_w, w_out):
    f32 = jnp.float32
    b, t, _ = h.shape
    z, xbc, dt = jnp.split(h @ w_in, [D_INNER, D_INNER + CONV_DIM], axis=-1)
    xpad = jnp.concatenate([s_conv.astype(xbc.dtype), xbc], axis=1)
    new_conv = xpad[:, t:]
    conv = conv_b.astype(f32) + sum(xpad[:, k:k + t].astype(f32) * conv_w[k].astype(f32) for k in range(D_CONV))
    xbc = jax.nn.silu(conv)
    xs, bm, cm = jnp.split(xbc, [D_INNER, D_INNER + N_GROUPS * D_STATE], axis=-1)
    xs = xs.reshape(b, t, H_C, HEAD_P)
    bm = bm.reshape(b, t, N_GROUPS, D_STATE)
    cm = cm.reshape(b, t, N_GROUPS, D_STATE)
    dt = jax.nn.softplus(dt.astype(f32) + dt_bias.astype(f32))
    a_neg = -jnp.exp(a_log.astype(f32))
    y, s_new = run_segments(functools.partial(ssd_scan, a_neg=a_neg), [xs, dt, bm, cm], s_ssm, seg)
    y = y + d_skip.astype(f32)[:, None] * xs
    y = y.reshape(b, t, D_INNER) * jax.nn.silu(z.astype(f32))
    y = rmsnorm(y.reshape(b, t, N_GROUPS, D_INNER // N_GROUPS), norm_w.reshape(N_GROUPS, D_INNER // N_GROUPS))
    return y.reshape(b, t, D_INNER).astype(h.dtype) @ w_out, s_new, new_conv


def swiglu(x, w_gate, w_up, w_down):
    return (jax.nn.silu(x @ w_gate) * (x @ w_up)) @ w_down


def trunk(x, seg, st_hgrn, st_gla, st_ssm, st_conv, lb_all,
          norm_mix_pre, norm_mix_post, norm_ffn_pre, norm_ffn_post,
          ev_w_in, ev_w_alpha_up, ev_b_alpha, ev_norm_a, ev_norm_b, ev_w_out,
          od_w_in, od_conv_w, od_conv_b, od_dt_bias, od_a_log, od_d_skip, od_norm, od_w_out,
          ffn_w_gate, ffn_w_up, ffn_w_down):
    new_hgrn, new_gla, new_ssm, new_conv = [], [], [], []
    for l in range(DEPTH):
        hn = rmsnorm(x, norm_mix_pre[l])
        if l % 2 == 0:
            e = l // 2
            mix, s_a, s_b = even_mixer(hn, st_hgrn[e], st_gla[e], seg, lb_all[l],
                                       ev_w_in[e], ev_w_alpha_up[e], ev_b_alpha[e],
                                       ev_norm_a[e], ev_norm_b[e], ev_w_out[e])
            new_hgrn.append(s_a)
            new_gla.append(s_b)
        else:
            o = l // 2
            mix, s_c, c_c = odd_mixer(hn, st_ssm[o], st_conv[o], seg, od_w_in[o], od_conv_w[o], od_conv_b[o],
                                      od_dt_bias[o], od_a_log[o], od_d_skip[o], od_norm[o], od_w_out[o])
            new_ssm.append(s_c)
            new_conv.append(c_c)
        x = x + rmsnorm(mix.astype(x.dtype), norm_mix_post[l])
        ff = swiglu(rmsnorm(x, norm_ffn_pre[l]), ffn_w_gate[l], ffn_w_up[l], ffn_w_down[l])
        x = x + rmsnorm(ff.astype(x.dtype), norm_ffn_post[l])
    return x, jnp.stack(new_hgrn), jnp.stack(new_gla), jnp.stack(new_ssm), jnp.stack(new_conv)


def setup_inputs(seed: int = 0) -> dict:
    key = jax.random.key(seed)
    ks = jax.random.split(key, 32)
    f32 = jnp.float32

    def nrm(k, shape, scale):
        return jax.random.normal(k, shape, f32) * scale

    def gain(k, shape):
        return 1.0 + 0.05 * jax.random.normal(k, shape, f32)

    dt0 = jnp.exp(jax.random.uniform(ks[21], (N_ODD, H_C), f32, math.log(1e-3), math.log(1e-1)))
    return {
        'x_prompt': nrm(ks[0], (BATCH, SEQ, D_MODEL), 1.0),
        'x_sample': nrm(ks[1], (DEC_BATCH, DEC_SEQ, D_MODEL), 1.0),
        'state_hgrn': nrm(ks[2], (N_EVEN, DEC_BATCH, H_A, DK_A, DV_A), 0.5),
        'state_gla': nrm(ks[3], (N_EVEN, DEC_BATCH, H_B, DK_B, DV_B), 0.5),
        'state_ssm': nrm(ks[4], (N_ODD, DEC_BATCH, H_C, HEAD_P, D_STATE), 0.5),
        'state_conv': nrm(ks[5], (N_ODD, DEC_BATCH, D_CONV - 1, CONV_DIM), 1.0),
        'meta_tokens': nrm(ks[6], (N_META, D_MODEL), 1.0),
        'hgrn_gamma': nrm(ks[7], (DEPTH + 1, H_A * DK_A), 0.5),
        'norm_mix_pre': gain(ks[8], (DEPTH, D_MODEL)),
        'norm_mix_post': gain(ks[9], (DEPTH, D_MODEL)),
        'norm_ffn_pre': gain(ks[10], (DEPTH, D_MODEL)),
        'norm_ffn_post': gain(ks[11], (DEPTH, D_MODEL)),
        'ev_w_in': nrm(ks[12], (N_EVEN, D_MODEL, IN_EVEN), D_MODEL ** -0.5),
        'ev_w_alpha_up': nrm(ks[13], (N_EVEN, GLA_RANK, H_B * DK_B), GLA_RANK ** -0.5),
        'ev_b_alpha': nrm(ks[14], (N_EVEN, H_B * DK_B), 0.1),
        'ev_norm_a': gain(ks[15], (N_EVEN, DV_A)),
        'ev_norm_b': gain(ks[16], (N_EVEN, DV_B)),
        'ev_w_out': nrm(ks[17], (N_EVEN, OUT_EVEN, D_MODEL), OUT_EVEN ** -0.5),
        'od_w_in': nrm(ks[18], (N_ODD, D_MODEL, IN_ODD), D_MODEL ** -0.5),
        'od_conv_w': nrm(ks[19], (N_ODD, D_CONV, CONV_DIM), D_CONV ** -0.5),
        'od_conv_b': nrm(ks[20], (N_ODD, CONV_DIM), 0.02),
        'od_dt_bias': dt0 + jnp.log(-jnp.expm1(-dt0)),
        'od_a_log': jnp.log(jax.random.uniform(ks[22], (N_ODD, H_C), f32, 1.0, 16.0)),
        'od_d_skip': gain(ks[23], (N_ODD, H_C)),
        'od_norm': gain(ks[24], (N_ODD, D_INNER)),
        'od_w_out': nrm(ks[25], (N_ODD, D_INNER, D_MODEL), D_INNER ** -0.5),
        'ffn_w_gate': nrm(ks[26], (DEPTH, D_MODEL, D_FF), D_MODEL ** -0.5),
        'ffn_w_up': nrm(ks[27], (DEPTH, D_MODEL, D_FF), D_MODEL ** -0.5),
        'ffn_w_down': nrm(ks[28], (DEPTH, D_FF, D_MODEL), D_FF ** -0.5),
    }


def reference(x_prompt, x_sample, state_hgrn, state_gla, state_ssm, state_conv, meta_tokens, hgrn_gamma,
              norm_mix_pre, norm_mix_post, norm_ffn_pre, norm_ffn_post,
              ev_w_in, ev_w_alpha_up, ev_b_alpha, ev_norm_a, ev_norm_b, ev_w_out,
              od_w_in, od_conv_w, od_conv_b, od_dt_bias, od_a_log, od_d_skip, od_norm, od_w_out,
              ffn_w_gate, ffn_w_up, ffn_w_down):
    f32 = jnp.float32
    lb_all = jnp.cumsum(jax.nn.softmax(hgrn_gamma.astype(f32), axis=0), axis=0)
    w = (norm_mix_pre, norm_mix_post, norm_ffn_pre, norm_ffn_post,
         ev_w_in, ev_w_alpha_up, ev_b_alpha, ev_norm_a, ev_norm_b, ev_w_out,
         od_w_in, od_conv_w, od_conv_b, od_dt_bias, od_a_log, od_d_skip, od_norm, od_w_out,
         ffn_w_gate, ffn_w_up, ffn_w_down)
    bp, sp = x_prompt.shape[0], x_prompt.shape[1]
    meta = jnp.broadcast_to(meta_tokens.astype(x_prompt.dtype)[None], (bp, N_META, D_MODEL))
    xp = jnp.concatenate([meta, x_prompt], axis=1)
    z_hgrn = jnp.zeros((N_EVEN, bp, H_A, DK_A, DV_A), f32)
    z_gla = jnp.zeros((N_EVEN, bp, H_B, DK_B, DV_B), f32)
    z_ssm = jnp.zeros((N_ODD, bp, H_C, HEAD_P, D_STATE), f32)
    z_conv = jnp.zeros((N_ODD, bp, D_CONV - 1, CONV_DIM), x_prompt.dtype)
    yp, hgrn_p, gla_p, ssm_p, conv_p = trunk(xp, (N_META, sp), z_hgrn, z_gla, z_ssm, z_conv, lb_all, *w)
    ys, hgrn_s, gla_s, ssm_s, conv_s = trunk(x_sample, (x_sample.shape[1],), state_hgrn, state_gla,
                                             state_ssm, state_conv, lb_all, *w)
    return (yp[:, N_META:], ys, hgrn_p, gla_p, ssm_p, conv_p, hgrn_s, gla_s, ssm_s, conv_s)
```

```python
import functools

import jax
import jax.numpy as jnp
from jax import lax
from jax.experimental import pallas as pl
from jax.experimental.pallas import tpu as pltpu

F32 = jnp.float32
BF16 = jnp.bfloat16
EPS = 1e-6
GLA_TAU = 16.0
LANES = 128
HALF = LANES // 2
VMEM_LIMIT_BYTES = 56 * 1024 * 1024
CONV_PAD = 8


def _cparams(n_grid):
    return pltpu.CompilerParams(dimension_semantics=("arbitrary",) * n_grid,
                                vmem_limit_bytes=VMEM_LIMIT_BYTES)


def _const_spec(shape):
    nd = len(shape)
    return pl.BlockSpec(shape, lambda *_: (0,) * nd, pipeline_mode=pl.Buffered(1))


def _dot(a, b):
    return jnp.dot(a, b, preferred_element_type=F32)


def _dot_nt(a, b):
    return lax.dot_general(a, b, (((1,), (1,)), ((), ())), preferred_element_type=F32)


def _dot_tn(a, b):
    return lax.dot_general(a, b, (((0,), (0,)), ((), ())), preferred_element_type=F32)


def _split3(x):
    hi = x.astype(BF16)
    r1 = x - hi.astype(F32)
    mid = r1.astype(BF16)
    lo = (r1 - mid.astype(F32)).astype(BF16)
    return hi, mid, lo


def _sel_left(m, x):
    hi, mid, lo = _split3(x)
    return (_dot(m, lo) + _dot(m, mid)) + _dot(m, hi)


def _sel_right(x, m):
    hi, mid, lo = _split3(x)
    return (_dot_nt(lo, m) + _dot_nt(mid, m)) + _dot_nt(hi, m)


def _rms(x, w):
    ms = jnp.mean(x * x, axis=-1, keepdims=True)
    return x * lax.rsqrt(ms + EPS) * w


def _silu(x):
    return x * jax.nn.sigmoid(x)


def _chunk_masks(rows, chunk):
    r = lax.broadcasted_iota(jnp.int32, (rows, rows), 0)
    c = lax.broadcasted_iota(jnp.int32, (rows, rows), 1)
    same = (r // chunk) == (c // chunk)
    causal = same & (c <= r)
    return same, causal


def _norm_proj_kernel(*refs, splits, n_tail, tail_t_rows):
    x_ref, g_ref, wm_ref = refs[:3]
    pos = 3
    wt_ref = wtt_ref = None
    if n_tail:
        wt_ref = refs[pos]
        pos += 1
    if tail_t_rows:
        wtt_ref = refs[pos]
        pos += 1
    outs = refs[pos:]
    hn = _rms(x_ref[...], g_ref[...]).astype(BF16)
    a = 0
    for k, w in enumerate(splits):
        outs[k][...] = _dot(hn, wm_ref[:, a:a + w])
        a += w
    k = len(splits)
    if n_tail:
        outs[k][...] = _dot(hn, wt_ref[...])
        k += 1
    if tail_t_rows:
        for r in range(hn.shape[0] // tail_t_rows):
            outs[k][r] = _dot_nt(wtt_ref[...], hn[r * tail_t_rows:(r + 1) * tail_t_rows])


def _norm_proj(x, g, w_main, splits, w_tail=None, w_tail_t=None, tail_t_rows=0, tile=512):
    n, d = x.shape
    tm = min(tile, n)
    assert n % tm == 0
    n_tail = 0 if w_tail is None else w_tail.shape[1]
    in_specs = [pl.BlockSpec((tm, d), lambda i: (i, 0)), _const_spec((1, d)), _const_spec(w_main.shape)]
    args = [x, g.reshape(1, d), w_main]
    out_shape = [jax.ShapeDtypeStruct((n, w), F32) for w in splits]
    out_specs = [pl.BlockSpec((tm, w), lambda i: (i, 0)) for w in splits]
    if n_tail:
        in_specs.append(_const_spec(w_tail.shape))
        args.append(w_tail)
        out_shape.append(jax.ShapeDtypeStruct((n, n_tail), F32))
        out_specs.append(pl.BlockSpec((tm, n_tail), lambda i: (i, 0)))
    if tail_t_rows:
        assert tm % tail_t_rows == 0
        in_specs.append(_const_spec(w_tail_t.shape))
        args.append(w_tail_t)
        out_shape.append(jax.ShapeDtypeStruct((n // tail_t_rows, n_tail, tail_t_rows), F32))
        out_specs.append(pl.BlockSpec((tm // tail_t_rows, n_tail, tail_t_rows), lambda i: (i, 0, 0)))
    return pl.pallas_call(
        functools.partial(_norm_proj_kernel, splits=tuple(splits), n_tail=n_tail, tail_t_rows=tail_t_rows),
        grid=(n // tm,), in_specs=in_specs, out_specs=out_specs, out_shape=out_shape,
        compiler_params=_cparams(1), name="norm_proj")(*args)


def _lin_core(qms, k_e, k_end, vs, tot, s_ref, s_idx, u_rows, causal, nseq, nchk, chunk):
    rows = nseq * nchk * chunk
    aligned = chunk % 8 == 0
    k_e16 = k_e.astype(BF16)
    k_end16 = k_end.astype(BF16) if aligned else None
    tot_t = tot.T
    if not aligned:
        row_chunk = lax.broadcasted_iota(jnp.int32, (rows, 1), 0) // chunk
    o_intra, q16, v16 = [], [], []
    for qm, v in zip(qms, vs):
        q16.append(qm.astype(BF16))
        v16.append(v.astype(BF16))
        scores = jnp.where(causal, _dot_nt(q16[-1], k_e16), 0.0)
        o_intra.append(_dot(scores.astype(BF16), v16[-1]))
    o_inter = [[] for _ in qms] if aligned else [jnp.zeros_like(o) for o in o_intra]
    for s in range(nseq):
        state = s_ref[s, s_idx]
        for n in range(nchk):
            g = s * nchk + n
            lo, hi = g * chunk, (g + 1) * chunk
            state16 = state.astype(BF16)
            upd = []
            for i in range(len(qms)):
                if aligned:
                    o_inter[i].append(_dot(q16[i][lo:hi], state16))
                    upd.append(_dot_tn(k_end16[lo:hi], v16[i][lo:hi]))
                else:
                    in_chunk = row_chunk == g
                    o_inter[i] = jnp.where(in_chunk, _dot(q16[i], state16), o_inter[i])
                    upd.append(_dot_tn(jnp.where(in_chunk, k_end, 0.0).astype(BF16), v16[i]))
            if len(upd) == 1:
                u = upd[0]
            else:
                u = jnp.concatenate([upd[i][a:b] for i, (a, b) in enumerate(u_rows)], axis=0)
            state = state * jnp.exp(tot_t[:, lo:lo + 1]) + u
        s_ref[s, s_idx] = state
    if aligned:
        o_inter = [jnp.concatenate(p, axis=0) if len(p) > 1 else p[0] for p in o_inter]
    return [a + b for a, b in zip(o_intra, o_inter)]


def _hgrn_kernel(q_ref, f_ref, i_ref, g_ref, gam_ref, nw_ref, s0_ref, o_ref, so_ref, *,
                 layer, heads, nseq, nchk, chunk):
    @pl.when(pl.program_id(1) == 0)
    def _():
        so_ref[...] = s0_ref[...]

    rows = nseq * nchk * chunk
    same, causal = _chunk_masks(rows, chunk)
    m_tri = jnp.where(causal, 1.0, 0.0).astype(BF16)
    m_same = jnp.where(same, 1.0, 0.0).astype(BF16)
    for h in range(heads):
        hs = slice(h * LANES, (h + 1) * LANES)
        gam = gam_ref[:, hs]
        e = jnp.exp(gam - jnp.max(gam, axis=0, keepdims=True))
        lb = jnp.sum(e[:layer + 1], axis=0, keepdims=True) / jnp.sum(e, axis=0, keepdims=True)
        f = lb + (1.0 - lb) * jax.nn.sigmoid(f_ref[:, hs])
        log_f = jnp.log(f)
        k = 1.0 - f
        cum = _sel_left(m_tri, log_f)
        tot = _sel_left(m_same, log_f)
        q_e = q_ref[:, hs] * jnp.exp(cum)
        k_e = k * jnp.exp(-cum)
        k_end = k * jnp.exp(tot - cum)
        (o,) = _lin_core([q_e], k_e, k_end, [i_ref[:, hs]], tot, so_ref, h, None, causal, nseq, nchk, chunk)
        o_ref[:, hs] = (_rms(o, nw_ref[...]) * _silu(g_ref[:, hs])).astype(o_ref.dtype)


def _gla_kernel(q_ref, k_ref, v_ref, g_ref, al_ref, wup_ref, bal_ref, nw_ref, s0_ref, o_ref, so_ref, *,
                pairs, nseq, nchk, chunk):
    @pl.when(pl.program_id(1) == 0)
    def _():
        so_ref[...] = s0_ref[...]

    rows = nseq * nchk * chunk
    same, causal = _chunk_masks(rows, chunk)
    m_tri = jnp.where(causal, 1.0, 0.0).astype(BF16)
    m_same = jnp.where(same, 1.0, 0.0).astype(BF16)
    log_alpha = jax.nn.log_sigmoid(_dot(al_ref[...].astype(BF16), wup_ref[...]) + bal_ref[...]) / GLA_TAU
    lane = lax.broadcasted_iota(jnp.int32, (rows, LANES), 1)
    scale = HALF ** -0.5
    for p in range(pairs):
        ps = slice(p * LANES, (p + 1) * LANES)
        log_f = log_alpha[:, ps]
        cum = _sel_left(m_tri, log_f)
        tot = _sel_left(m_same, log_f)
        q_e = (q_ref[:, ps] * scale) * jnp.exp(cum)
        k = k_ref[:, ps]
        k_e = k * jnp.exp(-cum)
        k_end = k * jnp.exp(tot - cum)
        qms = [jnp.where(lane < HALF, q_e, 0.0), jnp.where(lane >= HALF, q_e, 0.0)]
        hsl = [slice((2 * p + i) * LANES, (2 * p + i + 1) * LANES) for i in range(2)]
        outs = _lin_core(qms, k_e, k_end, [v_ref[:, hsl[0]], v_ref[:, hsl[1]]], tot, so_ref, p,
                         [(0, HALF), (HALF, LANES)], causal, nseq, nchk, chunk)
        for i in range(2):
            o_ref[:, hsl[i]] = (_rms(outs[i], nw_ref[...]) * _silu(g_ref[:, hsl[i]])).astype(o_ref.dtype)


def _seq_blocking(batch, t, max_rows, chunk_cap, max_seq=8):
    chunk = _gcd(t, chunk_cap)
    if t >= max_rows:
        assert t % max_rows == 0 and max_rows % chunk == 0
        return 1, max_rows, chunk
    nseq = max(1, min(batch, max_rows // t, max_seq))
    while batch % nseq:
        nseq -= 1
    return nseq, t, chunk


def _gcd(a, b):
    while b:
        a, b = b, a % b
    return a


def _hgrn_scan(pm, gamma, norm_w, s0, layer, batch, t, heads):
    n = pm.shape[0]
    width = heads * LANES
    nseq, t_blk, chunk = _seq_blocking(batch, t, 256, 32)
    rows = nseq * t_blk
    nj = t // t_blk
    bs0 = s0.shape[0]
    col = lambda c: pl.BlockSpec((rows, width), lambda i, j, c=c: (i * nj + j, c))
    s_block = (nseq,) + s0.shape[1:]
    s0_map = (lambda i, j: (i, 0, 0, 0)) if bs0 == batch else (lambda i, j: (0, 0, 0, 0))
    assert bs0 == batch or nseq == 1
    return pl.pallas_call(
        functools.partial(_hgrn_kernel, layer=layer, heads=heads, nseq=nseq, nchk=t_blk // chunk, chunk=chunk),
        grid=(batch // nseq, nj),
        in_specs=[col(0), col(1), col(2), col(3), _const_spec(gamma.shape), _const_spec((1, LANES)),
                  pl.BlockSpec(s_block, s0_map)],
        out_specs=[pl.BlockSpec((rows, width), lambda i, j: (i * nj + j, 0)),
                   pl.BlockSpec(s_block, lambda i, j: (i, 0, 0, 0))],
        out_shape=[jax.ShapeDtypeStruct((n, width), BF16),
                   jax.ShapeDtypeStruct((batch,) + s0.shape[1:], F32)],
        compiler_params=_cparams(2), name="hgrn_scan")(pm, pm, pm, pm, gamma, norm_w.reshape(1, LANES), s0)


def _gla_scan(pm, alow, w_up, b_al, norm_w, s0, batch, t, heads, col0):
    n = pm.shape[0]
    qk_w, v_w = heads * HALF, heads * LANES
    assert col0 % qk_w == 0 and (col0 + 2 * qk_w) % v_w == 0
    nseq, t_blk, chunk = _seq_blocking(batch, t, 256, 32)
    rows = nseq * t_blk
    nj = t // t_blk
    bs0 = s0.shape[0]
    cq = col0 // qk_w
    cv = (col0 + 2 * qk_w) // v_w
    blk = lambda w, c: pl.BlockSpec((rows, w), lambda i, j, c=c: (i * nj + j, c))
    s_block = (nseq,) + s0.shape[1:]
    s0_map = (lambda i, j: (i, 0, 0, 0)) if bs0 == batch else (lambda i, j: (0, 0, 0, 0))
    assert bs0 == batch or nseq == 1
    rank = alow.shape[1]
    return pl.pallas_call(
        functools.partial(_gla_kernel, pairs=heads // 2, nseq=nseq, nchk=t_blk // chunk, chunk=chunk),
        grid=(batch // nseq, nj),
        in_specs=[blk(qk_w, cq), blk(qk_w, cq + 1), blk(v_w, cv), blk(v_w, cv + 1), blk(rank, 0),
                  _const_spec(w_up.shape), _const_spec((1, qk_w)), _const_spec((1, LANES)),
                  pl.BlockSpec(s_block, s0_map)],
        out_specs=[pl.BlockSpec((rows, v_w), lambda i, j: (i * nj + j, 0)),
                   pl.BlockSpec(s_block, lambda i, j: (i, 0, 0, 0))],
        out_shape=[jax.ShapeDtypeStruct((n, v_w), BF16),
                   jax.ShapeDtypeStruct((batch,) + s0.shape[1:], F32)],
        compiler_params=_cparams(2), name="gla_scan")(
            pm, pm, pm, pm, alow, w_up, b_al.reshape(1, qk_w), norm_w.reshape(1, LANES), s0)


def _ssd_kernel(z_ref, xbc_ref, dtc_ref, dtr_ref, cw_ref, cb_ref, dtb_c_ref, dtb_r_ref, al_c_ref, al_r_ref,
                dsk_ref, nw_ref, cs0_ref, s0_ref, y_ref, cso_ref, so_ref, pad_ref, xc_ref, yb_ref, *,
                nseq, t_blk, heads, head_p, groups, d_state, d_conv):
    d_inner = heads * head_p
    hpg = heads // groups
    gw = hpg * head_p
    rows = nseq * t_blk
    hist = d_conv - 1
    h0 = CONV_PAD - hist

    @pl.when(pl.program_id(1) == 0)
    def _():
        so_ref[...] = s0_ref[...]
        for s in range(nseq):
            pad_ref[s, h0:CONV_PAD, :] = cs0_ref[s]

    for s in range(nseq):
        pad_ref[s, CONV_PAD:CONV_PAD + t_blk, :] = xbc_ref[s * t_blk:(s + 1) * t_blk, :]
        acc = pad_ref[s, h0:h0 + t_blk, :] * cw_ref[0:1, :]
        for k in range(1, d_conv):
            acc = acc + pad_ref[s, h0 + k:h0 + k + t_blk, :] * cw_ref[k:k + 1, :]
        xc_ref[s * t_blk:(s + 1) * t_blk, :] = _silu(cb_ref[...] + acc)
        tail = pad_ref[s, h0 + t_blk:CONV_PAD + t_blk, :]
        pad_ref[s, h0:CONV_PAD, :] = tail
        cso_ref[s] = tail

    same, causal = _chunk_masks(rows, t_blk)
    m_tri = jnp.where(causal, 1.0, 0.0).astype(BF16)
    m_same = jnp.where(same, 1.0, 0.0).astype(BF16)
    dt_c = jax.nn.softplus(dtc_ref[...] + dtb_c_ref[...])
    dt_r = jax.nn.softplus(dtr_ref[0] + dtb_r_ref[...])
    da_c = dt_c * (-jnp.exp(al_c_ref[...]))
    da_r = dt_r * (-jnp.exp(al_r_ref[...]))
    cum_c = _sel_left(m_tri, da_c)
    tot_c = _sel_left(m_same, da_c)
    cum_r = _sel_right(da_r, m_tri)
    tot_r = _sel_right(da_r, m_same)
    exp_cum = jnp.exp(cum_c)
    dec_end = jnp.exp(tot_c - cum_c)
    lane = lax.broadcasted_iota(jnp.int32, (rows, LANES), 1)
    low = lane < head_p
    row_seq = lax.broadcasted_iota(jnp.int32, (rows, 1), 0) // t_blk
    neg_inf = jnp.float32(-jnp.inf)

    def per_lane(col, ha):
        return jnp.where(low, col[:, ha:ha + 1], col[:, ha + 1:ha + 2])

    for g in range(groups):
        b16 = xc_ref[:, d_inner + g * d_state:d_inner + (g + 1) * d_state].astype(BF16)
        c16 = xc_ref[:, d_inner + (groups + g) * d_state:d_inner + (groups + g + 1) * d_state].astype(BF16)
        cb = _dot_nt(c16, b16)
        xe_parts, ec_parts = [], []
        for p in range(gw // LANES):
            pg = g * (gw // LANES) + p
            ha = pg * (LANES // head_p)
            ps = slice(pg * LANES, (pg + 1) * LANES)
            x2 = xc_ref[:, ps]
            xdt = x2 * per_lane(dt_c, ha)
            y2 = None
            for i in range(LANES // head_p):
                h = ha + i
                diff = cum_c[:, h:h + 1] - cum_r[h:h + 1, :]
                gmat = (cb * jnp.exp(jnp.where(causal, diff, neg_inf))).astype(BF16)
                xm = jnp.where(low if i == 0 else ~low, xdt, 0.0).astype(BF16)
                t = _dot(gmat, xm)
                y2 = t if y2 is None else y2 + t
            yb_ref[:, ps] = y2 + dsk_ref[:, ps] * x2
            xe_parts.append(xdt * per_lane(dec_end, ha))
            ec_parts.append(per_lane(exp_cum, ha))
        xe = jnp.concatenate(xe_parts, axis=1)
        ec = jnp.concatenate(ec_parts, axis=1)
        y_inter = jnp.zeros((rows, gw), F32)
        for s in range(nseq):
            state = so_ref[s, g]
            full = _dot_nt(c16, state.astype(BF16))
            if nseq == 1:
                y_inter = full
                u = _dot_tn(xe.astype(BF16), b16)
            else:
                in_seq = row_seq == s
                y_inter = jnp.where(in_seq, full, y_inter)
                u = _dot_tn(jnp.where(in_seq, xe, 0.0).astype(BF16), b16)
            for r in range(hpg):
                h = g * hpg + r
                rs = slice(r * head_p, (r + 1) * head_p)
                decay = jnp.exp(tot_r[h:h + 1, s * t_blk:s * t_blk + 1])
                so_ref[s, g, rs, :] = state[rs] * decay + u[rs]
        gs = slice(g * gw, (g + 1) * gw)
        yg = (yb_ref[:, gs] + y_inter * ec) * _silu(z_ref[:, gs])
        y_ref[:, gs] = _rms(yg, nw_ref[:, gs]).astype(y_ref.dtype)


def _ssd_scan(z, xbc, dt_c, dt_r, conv_w, conv_b, dt_bias, a_log, d_skip, norm_w, cs0, s0, batch, t,
              nseq, t_blk):
    n, d_inner = z.shape
    conv_dim = xbc.shape[1]
    heads = dt_c.shape[1]
    head_p = d_inner // heads
    groups, gw, d_state = s0.shape[1:]
    d_conv = conv_w.shape[0]
    assert gw * groups == d_inner and conv_dim == d_inner + 2 * groups * d_state
    assert d_state == LANES and LANES % head_p == 0 and gw % LANES == 0
    rows = nseq * t_blk
    nj = t // t_blk
    assert dt_r.shape == (n // rows, heads, rows)
    bs0 = s0.shape[0]
    assert bs0 == batch or nseq == 1
    bmap4 = (lambda i, j: (i, 0, 0, 0)) if bs0 == batch else (lambda i, j: (0, 0, 0, 0))
    bmap3 = (lambda i, j: (i, 0, 0)) if bs0 == batch else (lambda i, j: (0, 0, 0))
    rowmap = lambda i, j: (i * nj + j, 0)
    s_block = (nseq, groups, gw, d_state)
    c_block = (nseq, d_conv - 1, conv_dim)
    return pl.pallas_call(
        functools.partial(_ssd_kernel, nseq=nseq, t_blk=t_blk, heads=heads, head_p=head_p, groups=groups,
                          d_state=d_state, d_conv=d_conv),
        grid=(batch // nseq, nj),
        in_specs=[pl.BlockSpec((rows, d_inner), rowmap), pl.BlockSpec((rows, conv_dim), rowmap),
                  pl.BlockSpec((rows, heads), rowmap),
                  pl.BlockSpec((1, heads, rows), lambda i, j: (i * nj + j, 0, 0)),
                  _const_spec(conv_w.shape), _const_spec((1, conv_dim)),
                  _const_spec((1, heads)), _const_spec((heads, 1)), _const_spec((1, heads)), _const_spec((heads, 1)),
                  _const_spec((1, d_inner)), _const_spec((1, d_inner)),
                  pl.BlockSpec(c_block, bmap3), pl.BlockSpec(s_block, bmap4)],
        out_specs=[pl.BlockSpec((rows, d_inner), rowmap),
                   pl.BlockSpec(c_block, lambda i, j: (i, 0, 0)),
                   pl.BlockSpec(s_block, lambda i, j: (i, 0, 0, 0))],
        out_shape=[jax.ShapeDtypeStruct((n, d_inner), BF16),
                   jax.ShapeDtypeStruct((batch, d_conv - 1, conv_dim), F32),
                   jax.ShapeDtypeStruct((batch, groups, gw, d_state), F32)],
        scratch_shapes=[pltpu.VMEM((nseq, CONV_PAD + t_blk, conv_dim), F32),
                        pltpu.VMEM((rows, conv_dim), F32),
                        pltpu.VMEM((rows, d_inner), F32)],
        compiler_params=_cparams(2), name="ssd_scan")(
            z, xbc, dt_c, dt_r, conv_w, conv_b.reshape(1, conv_dim),
            dt_bias.reshape(1, heads), dt_bias.reshape(heads, 1), a_log.reshape(1, heads), a_log.reshape(heads, 1),
            jnp.repeat(d_skip, head_p).reshape(1, d_inner), norm_w.reshape(1, d_inner), cs0, s0)


def _post_ffn_kernel(*refs, n_mix, ff_chunk):
    x_ref = refs[0]
    o_refs = refs[1:1 + n_mix]
    wo_refs = refs[1 + n_mix:1 + 2 * n_mix]
    g_post, g_pre, wg_ref, wu_ref, wd_ref, g_fpost, y_ref = refs[1 + 2 * n_mix:]
    mix = _dot(o_refs[0][...], wo_refs[0][...])
    for o_ref, wo_ref in zip(o_refs[1:], wo_refs[1:]):
        mix = mix + _dot(o_ref[...], wo_ref[...])
    x1 = x_ref[...] + _rms(mix, g_post[...])
    h = _rms(x1, g_pre[...]).astype(BF16)
    d_ff = wg_ref.shape[1]
    ff = None
    for a in range(0, d_ff, ff_chunk):
        cs = slice(a, a + ff_chunk)
        act = (_silu(_dot(h, wg_ref[:, cs])) * _dot(h, wu_ref[:, cs])).astype(BF16)
        part = _dot(act, wd_ref[cs, :])
        ff = part if ff is None else ff + part
    y_ref[...] = x1 + _rms(ff, g_fpost[...])


def _post_ffn(x, mixes, w_outs, g_post, g_pre, w_gate, w_up, w_down, g_fpost, tile=256, ff_chunk=256):
    n, d = x.shape
    tm = min(tile, n)
    assert n % tm == 0 and w_gate.shape[1] % ff_chunk == 0
    row = lambda w: pl.BlockSpec((tm, w), lambda i: (i, 0))
    in_specs = ([row(d)] + [row(o.shape[1]) for o in mixes] + [_const_spec(w.shape) for w in w_outs]
                + [_const_spec((1, d)), _const_spec((1, d)), _const_spec(w_gate.shape), _const_spec(w_up.shape),
                   _const_spec(w_down.shape), _const_spec((1, d))])
    return pl.pallas_call(
        functools.partial(_post_ffn_kernel, n_mix=len(mixes), ff_chunk=ff_chunk),
        grid=(n // tm,), in_specs=in_specs, out_specs=row(d),
        out_shape=jax.ShapeDtypeStruct((n, d), F32),
        compiler_params=_cparams(1), name="post_ffn")(
            x, *mixes, *w_outs, g_post.reshape(1, d), g_pre.reshape(1, d), w_gate, w_up, w_down,
            g_fpost.reshape(1, d))


def _trunk(x, st, w):
    b, t, d = x.shape
    n = b * t
    xf = x.reshape(n, d)
    s_hgrn, s_gla, s_ssm, s_conv = st
    h_a, dk_a, dv_a = s_hgrn.shape[1:]
    h_b, dk_b, dv_b = s_gla.shape[1:]
    h_c, head_p, d_state = s_ssm.shape[1:]
    assert dk_a == LANES and dv_a == LANES and dk_b == HALF and dv_b == LANES and h_b % 2 == 0
    wa = h_a * LANES
    n_main0 = 4 * wa + 2 * h_b * dk_b + 2 * h_b * dv_b
    d_inner = h_c * head_p
    conv_dim = s_conv.shape[2]
    groups = (conv_dim - d_inner) // (2 * d_state)

    pm, alow = _norm_proj(xf, w["norm_mix_pre"][0], w["ev_w_in_main"], [n_main0], w_tail=w["ev_w_in_tail"])
    o_a, new_hgrn = _hgrn_scan(pm, w["hgrn_gamma"], w["ev_norm_a"], s_hgrn, 0, b, t, h_a)
    o_b, new_gla = _gla_scan(pm, alow, w["ev_w_alpha_up"], w["ev_b_alpha"], w["ev_norm_b"],
                             s_gla.reshape(s_gla.shape[0], h_b // 2, LANES, dv_b), b, t, h_b, 4 * wa)
    new_gla = new_gla.reshape(b, h_b, dk_b, dv_b)
    xf = _post_ffn(xf, [o_a, o_b], [w["ev_w_out_a"], w["ev_w_out_b"]], w["norm_mix_post"][0], w["norm_ffn_pre"][0],
                   w["ffn_w_gate"][0], w["ffn_w_up"][0], w["ffn_w_down"][0], w["norm_ffn_post"][0])

    nseq, t_blk, _ = _seq_blocking(b, t, LANES, LANES)
    z, xbc, dt_c, dt_r = _norm_proj(xf, w["norm_mix_pre"][1], w["od_w_in_main"], [d_inner, conv_dim],
                                    w_tail=w["od_w_in_tail"], w_tail_t=w["od_w_in_tail_t"],
                                    tail_t_rows=nseq * t_blk)
    y, new_conv, new_ssm = _ssd_scan(z, xbc, dt_c, dt_r, w["od_conv_w"], w["od_conv_b"], w["od_dt_bias"],
                                     w["od_a_log"], w["od_d_skip"], w["od_norm"], s_conv,
                                     s_ssm.reshape(s_ssm.shape[0], groups, (h_c // groups) * head_p, d_state),
                                     b, t, nseq, t_blk)
    new_ssm = new_ssm.reshape(b, h_c, head_p, d_state)
    xf = _post_ffn(xf, [y], [w["od_w_out"]], w["norm_mix_post"][1], w["norm_ffn_pre"][1],
                   w["ffn_w_gate"][1], w["ffn_w_up"][1], w["ffn_w_down"][1], w["norm_ffn_post"][1])
    return xf.reshape(b, t, d), (new_hgrn, new_gla, new_ssm, new_conv)


def kernel(x_prompt, x_sample, state_hgrn, state_gla, state_ssm, state_conv, meta_tokens, hgrn_gamma, norm_mix_pre, norm_mix_post, norm_ffn_pre, norm_ffn_post, ev_w_in, ev_w_alpha_up, ev_b_alpha, ev_norm_a, ev_norm_b, ev_w_out, od_w_in, od_conv_w, od_conv_b, od_dt_bias, od_a_log, od_d_skip, od_norm, od_w_out, ffn_w_gate, ffn_w_up, ffn_w_down):
    n_even, n_odd = state_hgrn.shape[0], state_ssm.shape[0]
    assert n_even == 1 and n_odd == 1 and norm_mix_pre.shape[0] == 2
    h_a, dk_a, dv_a = state_hgrn.shape[2:]
    h_b, dk_b, dv_b = state_gla.shape[2:]
    h_c, head_p, d_state = state_ssm.shape[2:]
    d_inner = h_c * head_p
    conv_dim = state_conv.shape[3]
    rank = ev_w_alpha_up.shape[1]
    n_main0 = ev_w_in.shape[2] - rank
    n_main1 = d_inner + conv_dim
    w_in0 = ev_w_in[0].astype(BF16)
    w_in1 = od_w_in[0].astype(BF16)
    w_out0 = ev_w_out[0].astype(BF16)
    w = {
        "hgrn_gamma": hgrn_gamma.astype(F32),
        "norm_mix_pre": norm_mix_pre, "norm_mix_post": norm_mix_post,
        "norm_ffn_pre": norm_ffn_pre, "norm_ffn_post": norm_ffn_post,
        "ev_w_in_main": w_in0[:, :n_main0], "ev_w_in_tail": w_in0[:, n_main0:],
        "ev_w_alpha_up": ev_w_alpha_up[0].astype(BF16), "ev_b_alpha": ev_b_alpha[0],
        "ev_norm_a": ev_norm_a[0], "ev_norm_b": ev_norm_b[0],
        "ev_w_out_a": w_out0[:h_a * dv_a], "ev_w_out_b": w_out0[h_a * dv_a:],
        "od_w_in_main": w_in1[:, :n_main1], "od_w_in_tail": w_in1[:, n_main1:],
        "od_w_in_tail_t": w_in1[:, n_main1:].T,
        "od_conv_w": od_conv_w[0], "od_conv_b": od_conv_b[0], "od_dt_bias": od_dt_bias[0],
        "od_a_log": od_a_log[0], "od_d_skip": od_d_skip[0], "od_norm": od_norm[0],
        "od_w_out": od_w_out[0].astype(BF16),
        "ffn_w_gate": ffn_w_gate.astype(BF16), "ffn_w_up": ffn_w_up.astype(BF16),
        "ffn_w_down": ffn_w_down.astype(BF16),
    }
    bp = x_prompt.shape[0]
    n_meta = meta_tokens.shape[0]
    zeros = (jnp.zeros((1, h_a, dk_a, dv_a), F32), jnp.zeros((1, h_b, dk_b, dv_b), F32),
             jnp.zeros((1, h_c, head_p, d_state), F32), jnp.zeros((1, state_conv.shape[2], conv_dim), F32))
    _, st_meta = _trunk(meta_tokens.astype(x_prompt.dtype)[None], zeros, w)
    yp, st_p = _trunk(x_prompt, st_meta, w)
    ys, st_s = _trunk(x_sample, (state_hgrn[0], state_gla[0], state_ssm[0], state_conv[0]), w)
    del bp, n_meta
    return (yp, ys, st_p[0][None], st_p[1][None], st_p[2][None], st_p[3][None],
            st_s[0][None], st_s[1][None], st_s[2][None], st_s[3][None])
```

```python
import functools

import jax
import jax.numpy as jnp
from jax import lax
from jax.experimental import pallas as pl
from jax.experimental.pallas import tpu as pltpu

F32 = jnp.float32
BF16 = jnp.bfloat16
EPS = 1e-6
GLA_TAU = 16.0
LN2 = 0.6931471805599453
LIN_BLOCK_ROWS = 256
LIN_CHUNK = 64
LANES = 128
HALF = LANES // 2
VMEM_LIMIT_BYTES = 56 * 1024 * 1024
CONV_PAD = 8


def _cparams(n_grid):
    return pltpu.CompilerParams(dimension_semantics=("arbitrary",) * n_grid,
                                vmem_limit_bytes=VMEM_LIMIT_BYTES)


def _const_spec(shape):
    nd = len(shape)
    return pl.BlockSpec(shape, lambda *_: (0,) * nd, pipeline_mode=pl.Buffered(1))


def _dot(a, b):
    return jnp.dot(a, b, preferred_element_type=F32)


def _dot_nt(a, b):
    return lax.dot_general(a, b, (((1,), (1,)), ((), ())), preferred_element_type=F32)


def _dot_tn(a, b):
    return lax.dot_general(a, b, (((0,), (0,)), ((), ())), preferred_element_type=F32)


def _split2(x):
    hi = x.astype(BF16)
    return hi, (x - hi.astype(F32)).astype(BF16)


def _split3(x):
    hi = x.astype(BF16)
    r1 = x - hi.astype(F32)
    mid = r1.astype(BF16)
    lo = (r1 - mid.astype(F32)).astype(BF16)
    return hi, mid, lo


def _sel_left(m, x):
    hi, mid, lo = _split3(x)
    return (_dot(m, lo) + _dot(m, mid)) + _dot(m, hi)


def _sel_right(x, m):
    hi, mid, lo = _split3(x)
    return (_dot_nt(lo, m) + _dot_nt(mid, m)) + _dot_nt(hi, m)


def _rms(x, w):
    ms = jnp.mean(x * x, axis=-1, keepdims=True)
    return x * lax.rsqrt(ms + EPS) * w


def _silu(x):
    h = 0.5 * x
    return h + h * jnp.tanh(h)


def _chunk_masks(rows, chunk):
    r = lax.broadcasted_iota(jnp.int32, (rows, rows), 0)
    c = lax.broadcasted_iota(jnp.int32, (rows, rows), 1)
    same = (r // chunk) == (c // chunk)
    causal = same & (c <= r)
    return same, causal


def _lock_step(gens):
    live = list(gens)
    while live:
        nxt = []
        for g in live:
            try:
                next(g)
                nxt.append(g)
            except StopIteration:
                pass
        live = nxt


def _norm_proj_kernel(*refs, splits, n_tail, tail_t_rows):
    x_ref, g_ref, wm_ref = refs[:3]
    pos = 3
    wt_ref = wtt_ref = None
    if n_tail:
        wt_ref = refs[pos]
        pos += 1
    if tail_t_rows:
        wtt_ref = refs[pos]
        pos += 1
    outs = refs[pos:]
    hn = _rms(x_ref[...], g_ref[...]).astype(BF16)
    a = 0
    for k, w in enumerate(splits):
        outs[k][...] = _dot(hn, wm_ref[:, a:a + w])
        a += w
    k = len(splits)
    if n_tail:
        outs[k][...] = _dot(hn, wt_ref[...])
        k += 1
    if tail_t_rows:
        for r in range(hn.shape[0] // tail_t_rows):
            outs[k][r] = _dot_nt(wtt_ref[...], hn[r * tail_t_rows:(r + 1) * tail_t_rows])


def _norm_proj(x, g, w_main, splits, w_tail=None, w_tail_t=None, tail_t_rows=0, tile=512):
    n, d = x.shape
    tm = min(tile, n)
    assert n % tm == 0
    n_tail = 0 if w_tail is None else w_tail.shape[1]
    in_specs = [pl.BlockSpec((tm, d), lambda i: (i, 0)), _const_spec((1, d)), _const_spec(w_main.shape)]
    args = [x, g.reshape(1, d), w_main]
    out_shape = [jax.ShapeDtypeStruct((n, w), F32) for w in splits]
    out_specs = [pl.BlockSpec((tm, w), lambda i: (i, 0)) for w in splits]
    if n_tail:
        in_specs.append(_const_spec(w_tail.shape))
        args.append(w_tail)
        out_shape.append(jax.ShapeDtypeStruct((n, n_tail), F32))
        out_specs.append(pl.BlockSpec((tm, n_tail), lambda i: (i, 0)))
    if tail_t_rows:
        assert tm % tail_t_rows == 0
        in_specs.append(_const_spec(w_tail_t.shape))
        args.append(w_tail_t)
        out_shape.append(jax.ShapeDtypeStruct((n // tail_t_rows, n_tail, tail_t_rows), F32))
        out_specs.append(pl.BlockSpec((tm // tail_t_rows, n_tail, tail_t_rows), lambda i: (i, 0, 0)))
    return pl.pallas_call(
        functools.partial(_norm_proj_kernel, splits=tuple(splits), n_tail=n_tail, tail_t_rows=tail_t_rows),
        grid=(n // tm,), in_specs=in_specs, out_specs=out_specs, out_shape=out_shape,
        compiler_params=_cparams(1), name="norm_proj")(*args)


def _sel2(m, x):
    hi, lo = _split2(x)
    c = x.shape[1]
    r = _dot(m, jnp.concatenate([hi, lo], axis=1))
    return r[:, :c] + r[:, c:]


def _chunk_rows(x, chunk, row):
    n = x.shape[0] // chunk
    parts = [jnp.broadcast_to(x[i * chunk + row:i * chunk + row + 1, :], (chunk, x.shape[1])) for i in range(n)]
    return parts[0] if n == 1 else jnp.concatenate(parts, axis=0)


def _lin_masks(rows, chunk):
    r = lax.broadcasted_iota(jnp.int32, (rows, rows), 0)
    c = lax.broadcasted_iota(jnp.int32, (rows, rows), 1)
    same = (r // chunk) == (c // chunk)
    causal = same & (c <= r)
    one = lambda m: jnp.where(m, 1.0, 0.0).astype(BF16)
    if chunk % 8 == 0:
        return causal, one(causal), None, None
    return causal, one(causal), one(same), one(same & ((c % chunk) < (chunk // 2)))


def _decays(log2_f, m_tri, m_same, m_mid, chunk):
    cum = _sel2(m_tri, log2_f)
    if chunk % 8 == 0:
        return cum, _chunk_rows(cum, chunk, chunk // 2 - 1), _chunk_rows(cum, chunk, chunk - 1)
    return cum, _sel2(m_mid, log2_f), _sel2(m_same, log2_f)


def _block_cols(x16, nblk, chunk, row_chunk):
    rows = x16.shape[0]
    parts = []
    for g in range(nblk):
        if chunk % 16 == 0:
            lo, hi = g * chunk, (g + 1) * chunk
            pieces = []
            if lo:
                pieces.append(jnp.zeros((lo, LANES), BF16))
            pieces.append(x16[lo:hi])
            if rows - hi:
                pieces.append(jnp.zeros((rows - hi, LANES), BF16))
            parts.append(pieces[0] if len(pieces) == 1 else jnp.concatenate(pieces, axis=0))
        else:
            parts.append(jnp.where(row_chunk == g, x16.astype(F32), 0.0).astype(BF16))
    return parts[0] if nblk == 1 else jnp.concatenate(parts, axis=1)


def _lin_core(q, k, v_list, key_masks, cum, mid, tot, st_ref, s_idx, causal, nseq, nchk, chunk):
    rows = nseq * nchk * chunk
    nblk = nseq * nchk
    row_chunk = lax.broadcasted_iota(jnp.int32, (rows, 1), 0) // chunk
    q_c = q * jnp.exp2(cum - mid)
    k_c = k * jnp.exp2(mid - cum)
    k_c16 = k_c.astype(BF16)
    k_blk = _block_cols((k_c * jnp.exp2(tot - mid)).astype(BF16), nblk, chunk, row_chunk)
    q_s = q_c * jnp.exp2(mid)
    dec = jnp.exp2(tot)
    scores, upd, vt16, q_blk = [], [], [], []
    for v, km in zip(v_list, key_masks):
        q_ci, q_si = (q_c, q_s) if km is None else (jnp.where(km, q_c, 0.0), jnp.where(km, q_s, 0.0))
        vt16.append(v.T.astype(BF16))
        scores.append(_dot_nt(q_ci.astype(BF16), k_c16))
        upd.append(_dot(vt16[-1], k_blk))
        q_blk.append(_block_cols(q_si.astype(BF16), nblk, chunk, row_chunk))
    yield
    if len(upd) == 1:
        u = upd[0]
    else:
        lane = lax.broadcasted_iota(jnp.int32, upd[0].shape, 1) % LANES
        u = jnp.where(lane < HALF, upd[0], upd[1])
    starts = []
    for s in range(nseq):
        st = st_ref[s, s_idx]
        for n in range(nchk):
            g = s * nchk + n
            starts.append(st.astype(BF16))
            st = st * dec[g * chunk:g * chunk + 1, :] + u[:, g * LANES:(g + 1) * LANES]
        st_ref[s, s_idx] = st
    st_all = starts[0] if nblk == 1 else jnp.concatenate(starts, axis=1)
    outs = []
    for i in range(len(v_list)):
        p = jnp.where(causal, scores[i], 0.0).astype(BF16)
        if rows % LANES == 0:
            outs.append(_dot_nt(jnp.concatenate([vt16[i], st_all], axis=1), jnp.concatenate([p, q_blk[i]], axis=1)))
        else:
            outs.append(_dot_nt(vt16[i], p) + _dot_nt(st_all, q_blk[i]))
    yield
    return outs


def _gated_norm_store(o_t, g_ref, nw_ref, o_ref, cols):
    o = o_t.T
    yield
    ms = jnp.mean(o * o, axis=-1, keepdims=True)
    gate = _silu(g_ref[:, cols]) * nw_ref[...]
    yield
    o_ref[:, cols] = (o * lax.rsqrt(ms + EPS) * gate).astype(o_ref.dtype)


def _hgrn_kernel(q_ref, f_ref, i_ref, g_ref, gam_ref, nw_ref, s0_ref, o_ref, so_ref, st_ref, *,
                 layer, heads, nseq, nchk, chunk):
    @pl.when(pl.program_id(1) == 0)
    def _():
        for s in range(nseq):
            for h in range(heads):
                st_ref[s, h] = s0_ref[s, h].T

    rows = nseq * nchk * chunk
    causal, m_tri, m_same, m_mid = _lin_masks(rows, chunk)

    def head(h):
        hs = slice(h * LANES, (h + 1) * LANES)
        gam = gam_ref[:, hs]
        e = jnp.exp(gam - jnp.max(gam, axis=0, keepdims=True))
        lb = jnp.sum(e[:layer + 1], axis=0, keepdims=True) / jnp.sum(e, axis=0, keepdims=True)
        half_gap = 0.5 * (1.0 - lb)
        f = (lb + half_gap) + half_gap * jnp.tanh(0.5 * f_ref[:, hs])
        cum, mid, tot = _decays(jnp.log2(f), m_tri, m_same, m_mid, chunk)
        yield
        (o_t,) = yield from _lin_core(q_ref[:, hs], 1.0 - f, [i_ref[:, hs]], [None], cum, mid, tot, st_ref, h,
                                      causal, nseq, nchk, chunk)
        yield from _gated_norm_store(o_t, g_ref, nw_ref, o_ref, hs)

    _lock_step([head(h) for h in range(heads)])

    @pl.when(pl.program_id(1) == pl.num_programs(1) - 1)
    def _():
        for s in range(nseq):
            for h in range(heads):
                so_ref[s, h] = st_ref[s, h].T


def _gla_kernel(q_ref, k_ref, v_ref, g_ref, al_ref, wup_ref, bal_ref, nw_ref, s0_ref, o_ref, so_ref, st_ref, *,
                pairs, nseq, nchk, chunk):
    @pl.when(pl.program_id(1) == 0)
    def _():
        for s in range(nseq):
            for p in range(pairs):
                st_ref[s, p] = s0_ref[s, p].T

    rows = nseq * nchk * chunk
    causal, m_tri, m_same, m_mid = _lin_masks(rows, chunk)
    log2_alpha = jax.nn.log_sigmoid(_dot(al_ref[...].astype(BF16), wup_ref[...]) + bal_ref[...]) * (
        1.0 / (GLA_TAU * LN2))
    lane = lax.broadcasted_iota(jnp.int32, (rows, LANES), 1)
    key_masks = [lane < HALF, lane >= HALF]
    scale = HALF ** -0.5

    def pair(p):
        ps = slice(p * LANES, (p + 1) * LANES)
        cum, mid, tot = _decays(log2_alpha[:, ps], m_tri, m_same, m_mid, chunk)
        yield
        hsl = [slice((2 * p + i) * LANES, (2 * p + i + 1) * LANES) for i in range(2)]
        outs = yield from _lin_core(q_ref[:, ps] * scale, k_ref[:, ps], [v_ref[:, hsl[0]], v_ref[:, hsl[1]]],
                                    key_masks, cum, mid, tot, st_ref, p, causal, nseq, nchk, chunk)
        yield from _gated_norm_store(outs[0], g_ref, nw_ref, o_ref, hsl[0])
        yield from _gated_norm_store(outs[1], g_ref, nw_ref, o_ref, hsl[1])

    _lock_step([pair(p) for p in range(pairs)])

    @pl.when(pl.program_id(1) == pl.num_programs(1) - 1)
    def _():
        for s in range(nseq):
            for p in range(pairs):
                so_ref[s, p] = st_ref[s, p].T


def _seq_blocking(batch, t, max_rows, chunk_cap, max_seq=8):
    chunk = _gcd(t, chunk_cap)
    if t >= max_rows:
        assert t % max_rows == 0 and max_rows % chunk == 0
        return 1, max_rows, chunk
    nseq = max(1, min(batch, max_rows // t, max_seq))
    while batch % nseq:
        nseq -= 1
    return nseq, t, chunk


def _gcd(a, b):
    while b:
        a, b = b, a % b
    return a


def _hgrn_scan(pm, gamma, norm_w, s0, layer, batch, t, heads):
    n = pm.shape[0]
    width = heads * LANES
    nseq, t_blk, chunk = _seq_blocking(batch, t, LIN_BLOCK_ROWS, LIN_CHUNK)
    rows = nseq * t_blk
    nj = t // t_blk
    bs0 = s0.shape[0]
    col = lambda c: pl.BlockSpec((rows, width), lambda i, j, c=c: (i * nj + j, c))
    s_block = (nseq,) + s0.shape[1:]
    s0_map = (lambda i, j: (i, 0, 0, 0)) if bs0 == batch else (lambda i, j: (0, 0, 0, 0))
    assert bs0 == batch or nseq == 1
    return pl.pallas_call(
        functools.partial(_hgrn_kernel, layer=layer, heads=heads, nseq=nseq, nchk=t_blk // chunk, chunk=chunk),
        grid=(batch // nseq, nj),
        in_specs=[col(0), col(1), col(2), col(3), _const_spec(gamma.shape), _const_spec((1, LANES)),
                  pl.BlockSpec(s_block, s0_map)],
        out_specs=[pl.BlockSpec((rows, width), lambda i, j: (i * nj + j, 0)),
                   pl.BlockSpec(s_block, lambda i, j: (i, 0, 0, 0))],
        out_shape=[jax.ShapeDtypeStruct((n, width), BF16),
                   jax.ShapeDtypeStruct((batch,) + s0.shape[1:], F32)],
        scratch_shapes=[pltpu.VMEM(s_block, F32)],
        compiler_params=_cparams(2), name="hgrn_scan")(pm, pm, pm, pm, gamma, norm_w.reshape(1, LANES), s0)


def _gla_scan(pm, alow, w_up, b_al, norm_w, s0, batch, t, heads, col0):
    n = pm.shape[0]
    qk_w, v_w = heads * HALF, heads * LANES
    assert col0 % qk_w == 0 and (col0 + 2 * qk_w) % v_w == 0
    nseq, t_blk, chunk = _seq_blocking(batch, t, LIN_BLOCK_ROWS, LIN_CHUNK)
    rows = nseq * t_blk
    nj = t // t_blk
    bs0 = s0.shape[0]
    cq = col0 // qk_w
    cv = (col0 + 2 * qk_w) // v_w
    blk = lambda w, c: pl.BlockSpec((rows, w), lambda i, j, c=c: (i * nj + j, c))
    s_block = (nseq,) + s0.shape[1:]
    s0_map = (lambda i, j: (i, 0, 0, 0)) if bs0 == batch else (lambda i, j: (0, 0, 0, 0))
    assert bs0 == batch or nseq == 1
    rank = alow.shape[1]
    return pl.pallas_call(
        functools.partial(_gla_kernel, pairs=heads // 2, nseq=nseq, nchk=t_blk // chunk, chunk=chunk),
        grid=(batch // nseq, nj),
        in_specs=[blk(qk_w, cq), blk(qk_w, cq + 1), blk(v_w, cv), blk(v_w, cv + 1), blk(rank, 0),
                  _const_spec(w_up.shape), _const_spec((1, qk_w)), _const_spec((1, LANES)),
                  pl.BlockSpec(s_block, s0_map)],
        out_specs=[pl.BlockSpec((rows, v_w), lambda i, j: (i * nj + j, 0)),
                   pl.BlockSpec(s_block, lambda i, j: (i, 0, 0, 0))],
        out_shape=[jax.ShapeDtypeStruct((n, v_w), BF16),
                   jax.ShapeDtypeStruct((batch,) + s0.shape[1:], F32)],
        scratch_shapes=[pltpu.VMEM(s_block, F32)],
        compiler_params=_cparams(2), name="gla_scan")(
            pm, pm, pm, pm, alow, w_up, b_al.reshape(1, qk_w), norm_w.reshape(1, LANES), s0)


def _ssd_kernel(z_ref, xbc_ref, dtc_ref, dtr_ref, cw_ref, cb_ref, dtb_c_ref, dtb_r_ref, al_c_ref, al_r_ref,
                dsk_ref, nw_ref, cs0_ref, s0_ref, y_ref, cso_ref, so_ref, pad_ref, xc_ref, yb_ref, *,
                nseq, t_blk, heads, head_p, groups, d_state, d_conv):
    d_inner = heads * head_p
    hpg = heads // groups
    gw = hpg * head_p
    rows = nseq * t_blk
    hist = d_conv - 1
    h0 = CONV_PAD - hist

    @pl.when(pl.program_id(1) == 0)
    def _():
        so_ref[...] = s0_ref[...]
        for s in range(nseq):
            pad_ref[s, h0:CONV_PAD, :] = cs0_ref[s]

    for s in range(nseq):
        pad_ref[s, CONV_PAD:CONV_PAD + t_blk, :] = xbc_ref[s * t_blk:(s + 1) * t_blk, :]
        acc = pad_ref[s, h0:h0 + t_blk, :] * cw_ref[0:1, :]
        for k in range(1, d_conv):
            acc = acc + pad_ref[s, h0 + k:h0 + k + t_blk, :] * cw_ref[k:k + 1, :]
        xc_ref[s * t_blk:(s + 1) * t_blk, :] = _silu(cb_ref[...] + acc)
        tail = pad_ref[s, h0 + t_blk:CONV_PAD + t_blk, :]
        pad_ref[s, h0:CONV_PAD, :] = tail
        cso_ref[s] = tail

    same, causal = _chunk_masks(rows, t_blk)
    m_tri = jnp.where(causal, 1.0, 0.0).astype(BF16)
    m_same = jnp.where(same, 1.0, 0.0).astype(BF16)
    dt_c = jax.nn.softplus(dtc_ref[...] + dtb_c_ref[...])
    dt_r = jax.nn.softplus(dtr_ref[0] + dtb_r_ref[...])
    da_c = dt_c * (-jnp.exp(al_c_ref[...]))
    da_r = dt_r * (-jnp.exp(al_r_ref[...]))
    cum_c = _sel_left(m_tri, da_c)
    tot_c = _sel_left(m_same, da_c)
    cum_r = _sel_right(da_r, m_tri)
    tot_r = _sel_right(da_r, m_same)
    exp_cum = jnp.exp(cum_c)
    dec_end = jnp.exp(tot_c - cum_c)
    lane = lax.broadcasted_iota(jnp.int32, (rows, LANES), 1)
    low = lane < head_p
    row_seq = lax.broadcasted_iota(jnp.int32, (rows, 1), 0) // t_blk
    neg_inf = jnp.float32(-jnp.inf)

    def per_lane(col, ha):
        return jnp.where(low, col[:, ha:ha + 1], col[:, ha + 1:ha + 2])

    for g in range(groups):
        b16 = xc_ref[:, d_inner + g * d_state:d_inner + (g + 1) * d_state].astype(BF16)
        c16 = xc_ref[:, d_inner + (groups + g) * d_state:d_inner + (groups + g + 1) * d_state].astype(BF16)
        cb = _dot_nt(c16, b16)
        xe_parts, ec_parts = [], []
        for p in range(gw // LANES):
            pg = g * (gw // LANES) + p
            ha = pg * (LANES // head_p)
            ps = slice(pg * LANES, (pg + 1) * LANES)
            x2 = xc_ref[:, ps]
            xdt = x2 * per_lane(dt_c, ha)
            y2 = None
            for i in range(LANES // head_p):
                h = ha + i
                diff = cum_c[:, h:h + 1] - cum_r[h:h + 1, :]
                gmat = (cb * jnp.exp(jnp.where(causal, diff, neg_inf))).astype(BF16)
                xm = jnp.where(low if i == 0 else ~low, xdt, 0.0).astype(BF16)
                t = _dot(gmat, xm)
                y2 = t if y2 is None else y2 + t
            yb_ref[:, ps] = y2 + dsk_ref[:, ps] * x2
            xe_parts.append(xdt * per_lane(dec_end, ha))
            ec_parts.append(per_lane(exp_cum, ha))
        xe = jnp.concatenate(xe_parts, axis=1)
        ec = jnp.concatenate(ec_parts, axis=1)
        y_inter = jnp.zeros((rows, gw), F32)
        for s in range(nseq):
            state = so_ref[s, g]
            full = _dot_nt(c16, state.astype(BF16))
            if nseq == 1:
                y_inter = full
                u = _dot_tn(xe.astype(BF16), b16)
            else:
                in_seq = row_seq == s
                y_inter = jnp.where(in_seq, full, y_inter)
                u = _dot_tn(jnp.where(in_seq, xe, 0.0).astype(BF16), b16)
            for r in range(hpg):
                h = g * hpg + r
                rs = slice(r * head_p, (r + 1) * head_p)
                decay = jnp.exp(tot_r[h:h + 1, s * t_blk:s * t_blk + 1])
                so_ref[s, g, rs, :] = state[rs] * decay + u[rs]
        gs = slice(g * gw, (g + 1) * gw)
        yg = (yb_ref[:, gs] + y_inter * ec) * _silu(z_ref[:, gs])
        y_ref[:, gs] = _rms(yg, nw_ref[:, gs]).astype(y_ref.dtype)


def _ssd_scan(z, xbc, dt_c, dt_r, conv_w, conv_b, dt_bias, a_log, d_skip, norm_w, cs0, s0, batch, t,
              nseq, t_blk):
    n, d_inner = z.shape
    conv_dim = xbc.shape[1]
    heads = dt_c.shape[1]
    head_p = d_inner // heads
    groups, gw, d_state = s0.shape[1:]
    d_conv = conv_w.shape[0]
    assert gw * groups == d_inner and conv_dim == d_inner + 2 * groups * d_state
    assert d_state == LANES and LANES % head_p == 0 and gw % LANES == 0
    rows = nseq * t_blk
    nj = t // t_blk
    assert dt_r.shape == (n // rows, heads, rows)
    bs0 = s0.shape[0]
    assert bs0 == batch or nseq == 1
    bmap4 = (lambda i, j: (i, 0, 0, 0)) if bs0 == batch else (lambda i, j: (0, 0, 0, 0))
    bmap3 = (lambda i, j: (i, 0, 0)) if bs0 == batch else (lambda i, j: (0, 0, 0))
    rowmap = lambda i, j: (i * nj + j, 0)
    s_block = (nseq, groups, gw, d_state)
    c_block = (nseq, d_conv - 1, conv_dim)
    return pl.pallas_call(
        functools.partial(_ssd_kernel, nseq=nseq, t_blk=t_blk, heads=heads, head_p=head_p, groups=groups,
                          d_state=d_state, d_conv=d_conv),
        grid=(batch // nseq, nj),
        in_specs=[pl.BlockSpec((rows, d_inner), rowmap), pl.BlockSpec((rows, conv_dim), rowmap),
                  pl.BlockSpec((rows, heads), rowmap),
                  pl.BlockSpec((1, heads, rows), lambda i, j: (i * nj + j, 0, 0)),
                  _const_spec(conv_w.shape), _const_spec((1, conv_dim)),
                  _const_spec((1, heads)), _const_spec((heads, 1)), _const_spec((1, heads)), _const_spec((heads, 1)),
                  _const_spec((1, d_inner)), _const_spec((1, d_inner)),
                  pl.BlockSpec(c_block, bmap3), pl.BlockSpec(s_block, bmap4)],
        out_specs=[pl.BlockSpec((rows, d_inner), rowmap),
                   pl.BlockSpec(c_block, lambda i, j: (i, 0, 0)),
                   pl.BlockSpec(s_block, lambda i, j: (i, 0, 0, 0))],
        out_shape=[jax.ShapeDtypeStruct((n, d_inner), BF16),
                   jax.ShapeDtypeStruct((batch, d_conv - 1, conv_dim), F32),
                   jax.ShapeDtypeStruct((batch, groups, gw, d_state), F32)],
        scratch_shapes=[pltpu.VMEM((nseq, CONV_PAD + t_blk, conv_dim), F32),
                        pltpu.VMEM((rows, conv_dim), F32),
                        pltpu.VMEM((rows, d_inner), F32)],
        compiler_params=_cparams(2), name="ssd_scan")(
            z, xbc, dt_c, dt_r, conv_w, conv_b.reshape(1, conv_dim),
            dt_bias.reshape(1, heads), dt_bias.reshape(heads, 1), a_log.reshape(1, heads), a_log.reshape(heads, 1),
            jnp.repeat(d_skip, head_p).reshape(1, d_inner), norm_w.reshape(1, d_inner), cs0, s0)


def _post_ffn_kernel(*refs, n_mix, ff_chunk):
    x_ref = refs[0]
    o_refs = refs[1:1 + n_mix]
    wo_refs = refs[1 + n_mix:1 + 2 * n_mix]
    g_post, g_pre, wg_ref, wu_ref, wd_ref, g_fpost, y_ref = refs[1 + 2 * n_mix:]
    mix = _dot(o_refs[0][...], wo_refs[0][...])
    for o_ref, wo_ref in zip(o_refs[1:], wo_refs[1:]):
        mix = mix + _dot(o_ref[...], wo_ref[...])
    x1 = x_ref[...] + _rms(mix, g_post[...])
    h = _rms(x1, g_pre[...]).astype(BF16)
    d_ff = wg_ref.shape[1]
    ff = None
    for a in range(0, d_ff, ff_chunk):
        cs = slice(a, a + ff_chunk)
        act = (_silu(_dot(h, wg_ref[:, cs])) * _dot(h, wu_ref[:, cs])).astype(BF16)
        part = _dot(act, wd_ref[cs, :])
        ff = part if ff is None else ff + part
    y_ref[...] = x1 + _rms(ff, g_fpost[...])


def _post_ffn(x, mixes, w_outs, g_post, g_pre, w_gate, w_up, w_down, g_fpost, tile=512, ff_chunk=256):
    n, d = x.shape
    tm = min(tile, n)
    assert n % tm == 0 and w_gate.shape[1] % ff_chunk == 0
    row = lambda w: pl.BlockSpec((tm, w), lambda i: (i, 0))
    in_specs = ([row(d)] + [row(o.shape[1]) for o in mixes] + [_const_spec(w.shape) for w in w_outs]
                + [_const_spec((1, d)), _const_spec((1, d)), _const_spec(w_gate.shape), _const_spec(w_up.shape),
                   _const_spec(w_down.shape), _const_spec((1, d))])
    return pl.pallas_call(
        functools.partial(_post_ffn_kernel, n_mix=len(mixes), ff_chunk=ff_chunk),
        grid=(n // tm,), in_specs=in_specs, out_specs=row(d),
        out_shape=jax.ShapeDtypeStruct((n, d), F32),
        compiler_params=_cparams(1), name="post_ffn")(
            x, *mixes, *w_outs, g_post.reshape(1, d), g_pre.reshape(1, d), w_gate, w_up, w_down,
            g_fpost.reshape(1, d))


def _trunk(x, st, w):
    b, t, d = x.shape
    n = b * t
    xf = x.reshape(n, d)
    s_hgrn, s_gla, s_ssm, s_conv = st
    h_a, dk_a, dv_a = s_hgrn.shape[1:]
    h_b, dk_b, dv_b = s_gla.shape[1:]
    h_c, head_p, d_state = s_ssm.shape[1:]
    assert dk_a == LANES and dv_a == LANES and dk_b == HALF and dv_b == LANES and h_b % 2 == 0
    wa = h_a * LANES
    n_main0 = 4 * wa + 2 * h_b * dk_b + 2 * h_b * dv_b
    d_inner = h_c * head_p
    conv_dim = s_conv.shape[2]
    groups = (conv_dim - d_inner) // (2 * d_state)

    pm, alow = _norm_proj(xf, w["norm_mix_pre"][0], w["ev_w_in_main"], [n_main0], w_tail=w["ev_w_in_tail"])
    o_a, new_hgrn = _hgrn_scan(pm, w["hgrn_gamma"], w["ev_norm_a"], s_hgrn, 0, b, t, h_a)
    o_b, new_gla = _gla_scan(pm, alow, w["ev_w_alpha_up"], w["ev_b_alpha"], w["ev_norm_b"],
                             s_gla.reshape(s_gla.shape[0], h_b // 2, LANES, dv_b), b, t, h_b, 4 * wa)
    new_gla = new_gla.reshape(b, h_b, dk_b, dv_b)
    xf = _post_ffn(xf, [o_a, o_b], [w["ev_w_out_a"], w["ev_w_out_b"]], w["norm_mix_post"][0], w["norm_ffn_pre"][0],
                   w["ffn_w_gate"][0], w["ffn_w_up"][0], w["ffn_w_down"][0], w["norm_ffn_post"][0])

    nseq, t_blk, _ = _seq_blocking(b, t, LANES, LANES)
    z, xbc, dt_c, dt_r = _norm_proj(xf, w["norm_mix_pre"][1], w["od_w_in_main"], [d_inner, conv_dim],
                                    w_tail=w["od_w_in_tail"], w_tail_t=w["od_w_in_tail_t"],
                                    tail_t_rows=nseq * t_blk)
    y, new_conv, new_ssm = _ssd_scan(z, xbc, dt_c, dt_r, w["od_conv_w"], w["od_conv_b"], w["od_dt_bias"],
                                     w["od_a_log"], w["od_d_skip"], w["od_norm"], s_conv,
                                     s_ssm.reshape(s_ssm.shape[0], groups, (h_c // groups) * head_p, d_state),
                                     b, t, nseq, t_blk)
    new_ssm = new_ssm.reshape(b, h_c, head_p, d_state)
    xf = _post_ffn(xf, [y], [w["od_w_out"]], w["norm_mix_post"][1], w["norm_ffn_pre"][1],
                   w["ffn_w_gate"][1], w["ffn_w_up"][1], w["ffn_w_down"][1], w["norm_ffn_post"][1])
    return xf.reshape(b, t, d), (new_hgrn, new_gla, new_ssm, new_conv)


def kernel(x_prompt, x_sample, state_hgrn, state_gla, state_ssm, state_conv, meta_tokens, hgrn_gamma, norm_mix_pre, norm_mix_post, norm_ffn_pre, norm_ffn_post, ev_w_in, ev_w_alpha_up, ev_b_alpha, ev_norm_a, ev_norm_b, ev_w_out, od_w_in, od_conv_w, od_conv_b, od_dt_bias, od_a_log, od_d_skip, od_norm, od_w_out, ffn_w_gate, ffn_w_up, ffn_w_down):
    n_even, n_odd = state_hgrn.shape[0], state_ssm.shape[0]
    assert n_even == 1 and n_odd == 1 and norm_mix_pre.shape[0] == 2
    h_a, dk_a, dv_a = state_hgrn.shape[2:]
    h_b, dk_b, dv_b = state_gla.shape[2:]
    h_c, head_p, d_state = state_ssm.shape[2:]
    d_inner = h_c * head_p
    conv_dim = state_conv.shape[3]
    rank = ev_w_alpha_up.shape[1]
    n_main0 = ev_w_in.shape[2] - rank
    n_main1 = d_inner + conv_dim
    cast = lambda a: a.astype(BF16)
    w = {
        "hgrn_gamma": hgrn_gamma.astype(F32),
        "norm_mix_pre": norm_mix_pre, "norm_mix_post": norm_mix_post,
        "norm_ffn_pre": norm_ffn_pre, "norm_ffn_post": norm_ffn_post,
        "ev_w_in_main": cast(ev_w_in[0, :, :n_main0]), "ev_w_in_tail": cast(ev_w_in[0, :, n_main0:]),
        "ev_w_alpha_up": ev_w_alpha_up[0].astype(BF16), "ev_b_alpha": ev_b_alpha[0],
        "ev_norm_a": ev_norm_a[0], "ev_norm_b": ev_norm_b[0],
        "ev_w_out_a": cast(ev_w_out[0, :h_a * dv_a]), "ev_w_out_b": cast(ev_w_out[0, h_a * dv_a:]),
        "od_w_in_main": cast(od_w_in[0, :, :n_main1]), "od_w_in_tail": cast(od_w_in[0, :, n_main1:]),
        "od_w_in_tail_t": cast(od_w_in[0, :, n_main1:].T),
        "od_conv_w": od_conv_w[0], "od_conv_b": od_conv_b[0], "od_dt_bias": od_dt_bias[0],
        "od_a_log": od_a_log[0], "od_d_skip": od_d_skip[0], "od_norm": od_norm[0],
        "od_w_out": od_w_out[0].astype(BF16),
        "ffn_w_gate": ffn_w_gate.astype(BF16), "ffn_w_up": ffn_w_up.astype(BF16),
        "ffn_w_down": ffn_w_down.astype(BF16),
    }
    bp = x_prompt.shape[0]
    n_meta = meta_tokens.shape[0]
    zeros = (jnp.zeros((1, h_a, dk_a, dv_a), F32), jnp.zeros((1, h_b, dk_b, dv_b), F32),
             jnp.zeros((1, h_c, head_p, d_state), F32), jnp.zeros((1, state_conv.shape[2], conv_dim), F32))
    _, st_meta = _trunk(meta_tokens.astype(x_prompt.dtype)[None], zeros, w)
    yp, st_p = _trunk(x_prompt, st_meta, w)
    ys, st_s = _trunk(x_sample, (state_hgrn[0], state_gla[0], state_ssm[0], state_conv[0]), w)
    del bp, n_meta
    return (yp, ys, st_p[0][None], st_p[1][None], st_p[2][None], st_p[3][None],
            st_s[0][None], st_s[1][None], st_s[2][None], st_s[3][None])
```

```python
import functools

import jax
import jax.numpy as jnp
from jax import lax
from jax.experimental import pallas as pl
from jax.experimental.pallas import tpu as pltpu

F32 = jnp.float32
BF16 = jnp.bfloat16
EPS = 1e-6
GLA_TAU = 16.0
LN2 = 0.6931471805599453
LIN_BLOCK_ROWS = 256
LIN_CHUNK = 64
LANES = 128
HALF = LANES // 2
VMEM_LIMIT_BYTES = 56 * 1024 * 1024
CONV_PAD = 8


def _cparams(n_grid):
    return pltpu.CompilerParams(dimension_semantics=("arbitrary",) * n_grid,
                                vmem_limit_bytes=VMEM_LIMIT_BYTES)


def _const_spec(shape):
    nd = len(shape)
    return pl.BlockSpec(shape, lambda *_: (0,) * nd, pipeline_mode=pl.Buffered(1))


def _dot(a, b):
    return jnp.dot(a, b, preferred_element_type=F32)


def _dot_nt(a, b):
    return lax.dot_general(a, b, (((1,), (1,)), ((), ())), preferred_element_type=F32)


def _dot_tn(a, b):
    return lax.dot_general(a, b, (((0,), (0,)), ((), ())), preferred_element_type=F32)


def _split2(x):
    hi = x.astype(BF16)
    return hi, (x - hi.astype(F32)).astype(BF16)


def _split3(x):
    hi = x.astype(BF16)
    r1 = x - hi.astype(F32)
    mid = r1.astype(BF16)
    lo = (r1 - mid.astype(F32)).astype(BF16)
    return hi, mid, lo


def _sel_left(m, x):
    hi, mid, lo = _split3(x)
    return (_dot(m, lo) + _dot(m, mid)) + _dot(m, hi)


def _sel_right(x, m):
    hi, mid, lo = _split3(x)
    return (_dot_nt(lo, m) + _dot_nt(mid, m)) + _dot_nt(hi, m)


def _rms(x, w):
    ms = jnp.mean(x * x, axis=-1, keepdims=True)
    return x * lax.rsqrt(ms + EPS) * w


def _silu(x):
    h = 0.5 * x
    return h + h * jnp.tanh(h)


def _chunk_masks(rows, chunk):
    r = lax.broadcasted_iota(jnp.int32, (rows, rows), 0)
    c = lax.broadcasted_iota(jnp.int32, (rows, rows), 1)
    same = (r // chunk) == (c // chunk)
    causal = same & (c <= r)
    return same, causal


def _lock_step(gens):
    live = list(gens)
    while live:
        nxt = []
        for g in live:
            try:
                next(g)
                nxt.append(g)
            except StopIteration:
                pass
        live = nxt


def _norm_proj_kernel(*refs, n_main, n_tail, tail_t_rows):
    x_ref, g_ref = refs[:2]
    wm_refs = refs[2:2 + n_main]
    pos = 2 + n_main
    wt_ref = wtt_ref = None
    if n_tail:
        wt_ref = refs[pos]
        pos += 1
    if tail_t_rows:
        wtt_ref = refs[pos]
        pos += 1
    outs = refs[pos:]
    hn = _rms(x_ref[...], g_ref[...]).astype(BF16)
    for k, wm_ref in enumerate(wm_refs):
        outs[k][...] = _dot(hn, wm_ref[...])
    k = n_main
    if n_tail:
        outs[k][...] = _dot(hn, wt_ref[...])
        k += 1
    if tail_t_rows:
        for r in range(hn.shape[0] // tail_t_rows):
            outs[k][r] = _dot_nt(wtt_ref[...], hn[r * tail_t_rows:(r + 1) * tail_t_rows])


def _norm_proj(x, g, w_mains, w_tail=None, w_tail_t=None, tail_t_rows=0, tile=512):
    n, d = x.shape
    tm = min(tile, n)
    assert n % tm == 0
    n_tail = 0 if w_tail is None else w_tail.shape[1]
    in_specs = [pl.BlockSpec((tm, d), lambda i: (i, 0)), _const_spec((1, d))] + [
        _const_spec(w.shape) for w in w_mains]
    args = [x, g.reshape(1, d)] + list(w_mains)
    out_shape = [jax.ShapeDtypeStruct((n, w.shape[1]), F32) for w in w_mains]
    out_specs = [pl.BlockSpec((tm, w.shape[1]), lambda i: (i, 0)) for w in w_mains]
    if n_tail:
        in_specs.append(_const_spec(w_tail.shape))
        args.append(w_tail)
        out_shape.append(jax.ShapeDtypeStruct((n, n_tail), F32))
        out_specs.append(pl.BlockSpec((tm, n_tail), lambda i: (i, 0)))
    if tail_t_rows:
        assert tm % tail_t_rows == 0
        in_specs.append(_const_spec(w_tail_t.shape))
        args.append(w_tail_t)
        out_shape.append(jax.ShapeDtypeStruct((n // tail_t_rows, n_tail, tail_t_rows), F32))
        out_specs.append(pl.BlockSpec((tm // tail_t_rows, n_tail, tail_t_rows), lambda i: (i, 0, 0)))
    return pl.pallas_call(
        functools.partial(_norm_proj_kernel, n_main=len(w_mains), n_tail=n_tail, tail_t_rows=tail_t_rows),
        grid=(n // tm,), in_specs=in_specs, out_specs=out_specs, out_shape=out_shape,
        compiler_params=_cparams(1), name="norm_proj")(*args)


def _sel2(m, x):
    hi, lo = _split2(x)
    c = x.shape[1]
    r = _dot(m, jnp.concatenate([hi, lo], axis=1))
    return r[:, :c] + r[:, c:]


def _chunk_rows(x, chunk, row):
    n = x.shape[0] // chunk
    parts = [jnp.broadcast_to(x[i * chunk + row:i * chunk + row + 1, :], (chunk, x.shape[1])) for i in range(n)]
    return parts[0] if n == 1 else jnp.concatenate(parts, axis=0)


def _lin_masks(rows, chunk):
    r = lax.broadcasted_iota(jnp.int32, (rows, rows), 0)
    c = lax.broadcasted_iota(jnp.int32, (rows, rows), 1)
    same = (r // chunk) == (c // chunk)
    causal = same & (c <= r)
    one = lambda m: jnp.where(m, 1.0, 0.0).astype(BF16)
    if chunk % 8 == 0:
        return causal, one(causal), None, None
    return causal, one(causal), one(same), one(same & ((c % chunk) < (chunk // 2)))


def _decays(log2_f, m_tri, m_same, m_mid, chunk):
    cum = _sel2(m_tri, log2_f)
    if chunk % 8 == 0:
        return cum, _chunk_rows(cum, chunk, chunk // 2 - 1), _chunk_rows(cum, chunk, chunk - 1)
    return cum, _sel2(m_mid, log2_f), _sel2(m_same, log2_f)


def _block_cols(x16, nblk, chunk, row_chunk):
    rows = x16.shape[0]
    parts = []
    for g in range(nblk):
        if chunk % 16 == 0:
            lo, hi = g * chunk, (g + 1) * chunk
            pieces = []
            if lo:
                pieces.append(jnp.zeros((lo, LANES), BF16))
            pieces.append(x16[lo:hi])
            if rows - hi:
                pieces.append(jnp.zeros((rows - hi, LANES), BF16))
            parts.append(pieces[0] if len(pieces) == 1 else jnp.concatenate(pieces, axis=0))
        else:
            parts.append(jnp.where(row_chunk == g, x16.astype(F32), 0.0).astype(BF16))
    return parts[0] if nblk == 1 else jnp.concatenate(parts, axis=1)


def _lin_core(q, k, v_list, key_masks, cum, mid, tot, st_ref, s_idx, causal, nseq, nchk, chunk):
    rows = nseq * nchk * chunk
    nblk = nseq * nchk
    row_chunk = lax.broadcasted_iota(jnp.int32, (rows, 1), 0) // chunk
    q_c = q * jnp.exp2(cum - mid)
    k_c = k * jnp.exp2(mid - cum)
    k_c16 = k_c.astype(BF16)
    k_blk = _block_cols((k_c * jnp.exp2(tot - mid)).astype(BF16), nblk, chunk, row_chunk)
    q_s = q_c * jnp.exp2(mid)
    dec = jnp.exp2(tot)
    scores, upd, vt16, q_blk = [], [], [], []
    for v, km in zip(v_list, key_masks):
        q_ci, q_si = (q_c, q_s) if km is None else (jnp.where(km, q_c, 0.0), jnp.where(km, q_s, 0.0))
        vt16.append(v.T.astype(BF16))
        scores.append(_dot_nt(q_ci.astype(BF16), k_c16))
        upd.append(_dot(vt16[-1], k_blk))
        q_blk.append(_block_cols(q_si.astype(BF16), nblk, chunk, row_chunk))
    yield
    if len(upd) == 1:
        u = upd[0]
    else:
        lane = lax.broadcasted_iota(jnp.int32, upd[0].shape, 1) % LANES
        u = jnp.where(lane < HALF, upd[0], upd[1])
    starts = []
    for s in range(nseq):
        st = st_ref[s, s_idx]
        for n in range(nchk):
            g = s * nchk + n
            starts.append(st.astype(BF16))
            st = st * dec[g * chunk:g * chunk + 1, :] + u[:, g * LANES:(g + 1) * LANES]
        st_ref[s, s_idx] = st
    st_all = starts[0] if nblk == 1 else jnp.concatenate(starts, axis=1)
    outs = []
    for i in range(len(v_list)):
        p = jnp.where(causal, scores[i], 0.0).astype(BF16)
        if rows % LANES == 0:
            outs.append(_dot_nt(jnp.concatenate([vt16[i], st_all], axis=1), jnp.concatenate([p, q_blk[i]], axis=1)))
        else:
            outs.append(_dot_nt(vt16[i], p) + _dot_nt(st_all, q_blk[i]))
    yield
    return outs


def _gated_norm_store(o_t, g_ref, nw_ref, o_ref, cols):
    o = o_t.T
    yield
    ms = jnp.mean(o * o, axis=-1, keepdims=True)
    gate = _silu(g_ref[:, cols]) * nw_ref[...]
    yield
    o_ref[:, cols] = (o * lax.rsqrt(ms + EPS) * gate).astype(o_ref.dtype)


def _hgrn_kernel(q_ref, f_ref, i_ref, g_ref, gam_ref, nw_ref, s0_ref, o_ref, so_ref, st_ref, *,
                 layer, heads, nseq, nchk, chunk):
    @pl.when(pl.program_id(1) == 0)
    def _():
        for s in range(nseq):
            for h in range(heads):
                st_ref[s, h] = s0_ref[s, h].T

    rows = nseq * nchk * chunk
    causal, m_tri, m_same, m_mid = _lin_masks(rows, chunk)

    def head(h):
        hs = slice(h * LANES, (h + 1) * LANES)
        gam = gam_ref[:, hs]
        e = jnp.exp(gam - jnp.max(gam, axis=0, keepdims=True))
        lb = jnp.sum(e[:layer + 1], axis=0, keepdims=True) / jnp.sum(e, axis=0, keepdims=True)
        half_gap = 0.5 * (1.0 - lb)
        f = (lb + half_gap) + half_gap * jnp.tanh(0.5 * f_ref[:, hs])
        cum, mid, tot = _decays(jnp.log2(f), m_tri, m_same, m_mid, chunk)
        yield
        (o_t,) = yield from _lin_core(q_ref[:, hs], 1.0 - f, [i_ref[:, hs]], [None], cum, mid, tot, st_ref, h,
                                      causal, nseq, nchk, chunk)
        yield from _gated_norm_store(o_t, g_ref, nw_ref, o_ref, hs)

    _lock_step([head(h) for h in range(heads)])

    @pl.when(pl.program_id(1) == pl.num_programs(1) - 1)
    def _():
        for s in range(nseq):
            for h in range(heads):
                so_ref[s, h] = st_ref[s, h].T


def _gla_kernel(q_ref, k_ref, v_ref, g_ref, al_ref, wup_ref, bal_ref, nw_ref, s0_ref, o_ref, so_ref, st_ref, *,
                pairs, nseq, nchk, chunk):
    @pl.when(pl.program_id(1) == 0)
    def _():
        for s in range(nseq):
            for p in range(pairs):
                st_ref[s, p] = s0_ref[s, p].T

    rows = nseq * nchk * chunk
    causal, m_tri, m_same, m_mid = _lin_masks(rows, chunk)
    log2_alpha = jax.nn.log_sigmoid(_dot(al_ref[...].astype(BF16), wup_ref[...]) + bal_ref[...]) * (
        1.0 / (GLA_TAU * LN2))
    lane = lax.broadcasted_iota(jnp.int32, (rows, LANES), 1)
    key_masks = [lane < HALF, lane >= HALF]
    scale = HALF ** -0.5

    def pair(p):
        ps = slice(p * LANES, (p + 1) * LANES)
        cum, mid, tot = _decays(log2_alpha[:, ps], m_tri, m_same, m_mid, chunk)
        yield
        hsl = [slice((2 * p + i) * LANES, (2 * p + i + 1) * LANES) for i in range(2)]
        outs = yield from _lin_core(q_ref[:, ps] * scale, k_ref[:, ps], [v_ref[:, hsl[0]], v_ref[:, hsl[1]]],
                                    key_masks, cum, mid, tot, st_ref, p, causal, nseq, nchk, chunk)
        yield from _gated_norm_store(outs[0], g_ref, nw_ref, o_ref, hsl[0])
        yield from _gated_norm_store(outs[1], g_ref, nw_ref, o_ref, hsl[1])

    _lock_step([pair(p) for p in range(pairs)])

    @pl.when(pl.program_id(1) == pl.num_programs(1) - 1)
    def _():
        for s in range(nseq):
            for p in range(pairs):
                so_ref[s, p] = st_ref[s, p].T


def _seq_blocking(batch, t, max_rows, chunk_cap, max_seq=8):
    chunk = _gcd(t, chunk_cap)
    if t >= max_rows:
        assert t % max_rows == 0 and max_rows % chunk == 0
        return 1, max_rows, chunk
    nseq = max(1, min(batch, max_rows // t, max_seq))
    while batch % nseq:
        nseq -= 1
    return nseq, t, chunk


def _gcd(a, b):
    while b:
        a, b = b, a % b
    return a


def _hgrn_scan(pm, gamma, norm_w, s0, layer, batch, t, heads):
    n = pm.shape[0]
    width = heads * LANES
    nseq, t_blk, chunk = _seq_blocking(batch, t, LIN_BLOCK_ROWS, LIN_CHUNK)
    rows = nseq * t_blk
    nj = t // t_blk
    bs0 = s0.shape[0]
    col = lambda c: pl.BlockSpec((rows, width), lambda i, j, c=c: (i * nj + j, c))
    s_block = (nseq,) + s0.shape[1:]
    s0_map = (lambda i, j: (i, 0, 0, 0)) if bs0 == batch else (lambda i, j: (0, 0, 0, 0))
    assert bs0 == batch or nseq == 1
    return pl.pallas_call(
        functools.partial(_hgrn_kernel, layer=layer, heads=heads, nseq=nseq, nchk=t_blk // chunk, chunk=chunk),
        grid=(batch // nseq, nj),
        in_specs=[col(0), col(1), col(2), col(3), _const_spec(gamma.shape), _const_spec((1, LANES)),
                  pl.BlockSpec(s_block, s0_map)],
        out_specs=[pl.BlockSpec((rows, width), lambda i, j: (i * nj + j, 0)),
                   pl.BlockSpec(s_block, lambda i, j: (i, 0, 0, 0))],
        out_shape=[jax.ShapeDtypeStruct((n, width), BF16),
                   jax.ShapeDtypeStruct((batch,) + s0.shape[1:], F32)],
        scratch_shapes=[pltpu.VMEM(s_block, F32)],
        compiler_params=_cparams(2), name="hgrn_scan")(pm, pm, pm, pm, gamma, norm_w.reshape(1, LANES), s0)


def _gla_scan(pm, alow, w_up, b_al, norm_w, s0, batch, t, heads, col0):
    n = pm.shape[0]
    qk_w, v_w = heads * HALF, heads * LANES
    assert col0 % qk_w == 0 and (col0 + 2 * qk_w) % v_w == 0
    nseq, t_blk, chunk = _seq_blocking(batch, t, LIN_BLOCK_ROWS, LIN_CHUNK)
    rows = nseq * t_blk
    nj = t // t_blk
    bs0 = s0.shape[0]
    cq = col0 // qk_w
    cv = (col0 + 2 * qk_w) // v_w
    blk = lambda w, c: pl.BlockSpec((rows, w), lambda i, j, c=c: (i * nj + j, c))
    s_block = (nseq,) + s0.shape[1:]
    s0_map = (lambda i, j: (i, 0, 0, 0)) if bs0 == batch else (lambda i, j: (0, 0, 0, 0))
    assert bs0 == batch or nseq == 1
    rank = alow.shape[1]
    return pl.pallas_call(
        functools.partial(_gla_kernel, pairs=heads // 2, nseq=nseq, nchk=t_blk // chunk, chunk=chunk),
        grid=(batch // nseq, nj),
        in_specs=[blk(qk_w, cq), blk(qk_w, cq + 1), blk(v_w, cv), blk(v_w, cv + 1), blk(rank, 0),
                  _const_spec(w_up.shape), _const_spec((1, qk_w)), _const_spec((1, LANES)),
                  pl.BlockSpec(s_block, s0_map)],
        out_specs=[pl.BlockSpec((rows, v_w), lambda i, j: (i * nj + j, 0)),
                   pl.BlockSpec(s_block, lambda i, j: (i, 0, 0, 0))],
        out_shape=[jax.ShapeDtypeStruct((n, v_w), BF16),
                   jax.ShapeDtypeStruct((batch,) + s0.shape[1:], F32)],
        scratch_shapes=[pltpu.VMEM(s_block, F32)],
        compiler_params=_cparams(2), name="gla_scan")(
            pm, pm, pm, pm, alow, w_up, b_al.reshape(1, qk_w), norm_w.reshape(1, LANES), s0)


def _ssd_kernel(z_ref, xbc_ref, dtc_ref, dtr_ref, cw_ref, cb_ref, dtb_c_ref, dtb_r_ref, al_c_ref, al_r_ref,
                dsk_ref, nw_ref, cs0_ref, s0_ref, y_ref, cso_ref, so_ref, pad_ref, xc_ref, yb_ref, *,
                nseq, t_blk, heads, head_p, groups, d_state, d_conv):
    d_inner = heads * head_p
    hpg = heads // groups
    gw = hpg * head_p
    rows = nseq * t_blk
    hist = d_conv - 1
    h0 = CONV_PAD - hist

    @pl.when(pl.program_id(1) == 0)
    def _():
        so_ref[...] = s0_ref[...]
        for s in range(nseq):
            pad_ref[s, h0:CONV_PAD, :] = cs0_ref[s]

    for s in range(nseq):
        pad_ref[s, CONV_PAD:CONV_PAD + t_blk, :] = xbc_ref[s * t_blk:(s + 1) * t_blk, :]
        acc = pad_ref[s, h0:h0 + t_blk, :] * cw_ref[0:1, :]
        for k in range(1, d_conv):
            acc = acc + pad_ref[s, h0 + k:h0 + k + t_blk, :] * cw_ref[k:k + 1, :]
        xc_ref[s * t_blk:(s + 1) * t_blk, :] = _silu(cb_ref[...] + acc)
        tail = pad_ref[s, h0 + t_blk:CONV_PAD + t_blk, :]
        pad_ref[s, h0:CONV_PAD, :] = tail
        cso_ref[s] = tail

    same, causal = _chunk_masks(rows, t_blk)
    m_tri = jnp.where(causal, 1.0, 0.0).astype(BF16)
    m_same = jnp.where(same, 1.0, 0.0).astype(BF16)
    dt_c = jax.nn.softplus(dtc_ref[...] + dtb_c_ref[...])
    dt_r = jax.nn.softplus(dtr_ref[0] + dtb_r_ref[...])
    da_c = dt_c * (-jnp.exp(al_c_ref[...]))
    da_r = dt_r * (-jnp.exp(al_r_ref[...]))
    cum_c = _sel_left(m_tri, da_c)
    tot_c = _sel_left(m_same, da_c)
    cum_r = _sel_right(da_r, m_tri)
    tot_r = _sel_right(da_r, m_same)
    exp_cum = jnp.exp(cum_c)
    dec_end = jnp.exp(tot_c - cum_c)
    lane = lax.broadcasted_iota(jnp.int32, (rows, LANES), 1)
    low = lane < head_p
    row_seq = lax.broadcasted_iota(jnp.int32, (rows, 1), 0) // t_blk
    neg_inf = jnp.float32(-jnp.inf)

    def per_lane(col, ha):
        return jnp.where(low, col[:, ha:ha + 1], col[:, ha + 1:ha + 2])

    for g in range(groups):
        b16 = xc_ref[:, d_inner + g * d_state:d_inner + (g + 1) * d_state].astype(BF16)
        c16 = xc_ref[:, d_inner + (groups + g) * d_state:d_inner + (groups + g + 1) * d_state].astype(BF16)
        cb = _dot_nt(c16, b16)
        xe_parts, ec_parts = [], []
        for p in range(gw // LANES):
            pg = g * (gw // LANES) + p
            ha = pg * (LANES // head_p)
            ps = slice(pg * LANES, (pg + 1) * LANES)
            x2 = xc_ref[:, ps]
            xdt = x2 * per_lane(dt_c, ha)
            y2 = None
            for i in range(LANES // head_p):
                h = ha + i
                diff = cum_c[:, h:h + 1] - cum_r[h:h + 1, :]
                gmat = (cb * jnp.exp(jnp.where(causal, diff, neg_inf))).astype(BF16)
                xm = jnp.where(low if i == 0 else ~low, xdt, 0.0).astype(BF16)
                t = _dot(gmat, xm)
                y2 = t if y2 is None else y2 + t
            yb_ref[:, ps] = y2 + dsk_ref[:, ps] * x2
            xe_parts.append(xdt * per_lane(dec_end, ha))
            ec_parts.append(per_lane(exp_cum, ha))
        xe = jnp.concatenate(xe_parts, axis=1)
        ec = jnp.concatenate(ec_parts, axis=1)
        y_inter = jnp.zeros((rows, gw), F32)
        for s in range(nseq):
            state = so_ref[s, g]
            full = _dot_nt(c16, state.astype(BF16))
            if nseq == 1:
                y_inter = full
                u = _dot_tn(xe.astype(BF16), b16)
            else:
                in_seq = row_seq == s
                y_inter = jnp.where(in_seq, full, y_inter)
                u = _dot_tn(jnp.where(in_seq, xe, 0.0).astype(BF16), b16)
            for r in range(hpg):
                h = g * hpg + r
                rs = slice(r * head_p, (r + 1) * head_p)
                decay = jnp.exp(tot_r[h:h + 1, s * t_blk:s * t_blk + 1])
                so_ref[s, g, rs, :] = state[rs] * decay + u[rs]
        gs = slice(g * gw, (g + 1) * gw)
        yg = (yb_ref[:, gs] + y_inter * ec) * _silu(z_ref[:, gs])
        y_ref[:, gs] = _rms(yg, nw_ref[:, gs]).astype(y_ref.dtype)


def _ssd_scan(z, xbc, dt_c, dt_r, conv_w, conv_b, dt_bias, a_log, d_skip, norm_w, cs0, s0, batch, t,
              nseq, t_blk):
    n, d_inner = z.shape
    conv_dim = xbc.shape[1]
    heads = dt_c.shape[1]
    head_p = d_inner // heads
    groups, gw, d_state = s0.shape[1:]
    d_conv = conv_w.shape[0]
    assert gw * groups == d_inner and conv_dim == d_inner + 2 * groups * d_state
    assert d_state == LANES and LANES % head_p == 0 and gw % LANES == 0
    rows = nseq * t_blk
    nj = t // t_blk
    assert dt_r.shape == (n // rows, heads, rows)
    bs0 = s0.shape[0]
    assert bs0 == batch or nseq == 1
    bmap4 = (lambda i, j: (i, 0, 0, 0)) if bs0 == batch else (lambda i, j: (0, 0, 0, 0))
    bmap3 = (lambda i, j: (i, 0, 0)) if bs0 == batch else (lambda i, j: (0, 0, 0))
    rowmap = lambda i, j: (i * nj + j, 0)
    s_block = (nseq, groups, gw, d_state)
    c_block = (nseq, d_conv - 1, conv_dim)
    return pl.pallas_call(
        functools.partial(_ssd_kernel, nseq=nseq, t_blk=t_blk, heads=heads, head_p=head_p, groups=groups,
                          d_state=d_state, d_conv=d_conv),
        grid=(batch // nseq, nj),
        in_specs=[pl.BlockSpec((rows, d_inner), rowmap), pl.BlockSpec((rows, conv_dim), rowmap),
                  pl.BlockSpec((rows, heads), rowmap),
                  pl.BlockSpec((1, heads, rows), lambda i, j: (i * nj + j, 0, 0)),
                  _const_spec(conv_w.shape), _const_spec((1, conv_dim)),
                  _const_spec((1, heads)), _const_spec((heads, 1)), _const_spec((1, heads)), _const_spec((heads, 1)),
                  _const_spec((1, d_inner)), _const_spec((1, d_inner)),
                  pl.BlockSpec(c_block, bmap3), pl.BlockSpec(s_block, bmap4)],
        out_specs=[pl.BlockSpec((rows, d_inner), rowmap),
                   pl.BlockSpec(c_block, lambda i, j: (i, 0, 0)),
                   pl.BlockSpec(s_block, lambda i, j: (i, 0, 0, 0))],
        out_shape=[jax.ShapeDtypeStruct((n, d_inner), BF16),
                   jax.ShapeDtypeStruct((batch, d_conv - 1, conv_dim), F32),
                   jax.ShapeDtypeStruct((batch, groups, gw, d_state), F32)],
        scratch_shapes=[pltpu.VMEM((nseq, CONV_PAD + t_blk, conv_dim), F32),
                        pltpu.VMEM((rows, conv_dim), F32),
                        pltpu.VMEM((rows, d_inner), F32)],
        compiler_params=_cparams(2), name="ssd_scan")(
            z, xbc, dt_c, dt_r, conv_w, conv_b.reshape(1, conv_dim),
            dt_bias.reshape(1, heads), dt_bias.reshape(heads, 1), a_log.reshape(1, heads), a_log.reshape(heads, 1),
            jnp.repeat(d_skip, head_p).reshape(1, d_inner), norm_w.reshape(1, d_inner), cs0, s0)


def _ssd_proj_kernel(x_ref, g_ref, wz_ref, wx_ref, wt_ref, wtt_ref, cw_ref, cb_ref, cs0_ref,
                     zs_ref, xc_ref, dtc_ref, dtr_ref, cso_ref, pad_ref, *, tiles_per_seq, t_rows, d_conv,
                     shared_state):
    i = pl.program_id(0)
    tm = x_ref.shape[0]
    h0 = CONV_PAD - (d_conv - 1)

    @pl.when(i % tiles_per_seq == 0)
    def _():
        pad_ref[h0:CONV_PAD, :] = cs0_ref[0] if shared_state else cs0_ref[i // tiles_per_seq]

    hn = _rms(x_ref[...], g_ref[...]).astype(BF16)
    pad_ref[CONV_PAD:CONV_PAD + tm, :] = _dot(hn, wx_ref[...])
    zs_ref[...] = _silu(_dot(hn, wz_ref[...]))
    dtc_ref[...] = _dot(hn, wt_ref[...])
    for r in range(tm // t_rows):
        dtr_ref[r] = _dot_nt(wtt_ref[...], hn[r * t_rows:(r + 1) * t_rows])
    acc = pad_ref[h0:h0 + tm, :] * cw_ref[0:1, :]
    for k in range(1, d_conv):
        acc = acc + pad_ref[h0 + k:h0 + k + tm, :] * cw_ref[k:k + 1, :]
    xc_ref[...] = _silu(cb_ref[...] + acc)
    tail = pad_ref[h0 + tm:CONV_PAD + tm, :]
    pad_ref[h0:CONV_PAD, :] = tail
    cso_ref[0] = tail


def _ssd_proj(x, g, w_z, w_x, w_t2, w_tt, conv_w, conv_b, cs0, batch, t, t_rows, tile=512):
    n, d = x.shape
    tm = min(tile, t)
    assert t % tm == 0 and tm % t_rows == 0
    tps = t // tm
    d_inner, conv_dim = w_z.shape[1], w_x.shape[1]
    heads = w_tt.shape[0]
    d_conv = conv_w.shape[0]
    row = lambda w: pl.BlockSpec((tm, w), lambda i: (i, 0))
    return pl.pallas_call(
        functools.partial(_ssd_proj_kernel, tiles_per_seq=tps, t_rows=t_rows, d_conv=d_conv,
                          shared_state=cs0.shape[0] != batch),
        grid=(n // tm,),
        in_specs=[row(d), _const_spec((1, d)), _const_spec(w_z.shape), _const_spec(w_x.shape),
                  _const_spec(w_t2.shape), _const_spec(w_tt.shape), _const_spec(conv_w.shape),
                  _const_spec((1, conv_dim)), _const_spec(cs0.shape)],
        out_specs=[row(d_inner), row(conv_dim), row(2 * heads),
                   pl.BlockSpec((tm // t_rows, heads, t_rows), lambda i: (i, 0, 0)),
                   pl.BlockSpec((1, d_conv - 1, conv_dim), lambda i: (i // tps, 0, 0))],
        out_shape=[jax.ShapeDtypeStruct((n, d_inner), F32), jax.ShapeDtypeStruct((n, conv_dim), F32),
                   jax.ShapeDtypeStruct((n, 2 * heads), F32),
                   jax.ShapeDtypeStruct((n // t_rows, heads, t_rows), F32),
                   jax.ShapeDtypeStruct((batch, d_conv - 1, conv_dim), F32)],
        scratch_shapes=[pltpu.VMEM((CONV_PAD + tm, conv_dim), F32)],
        compiler_params=_cparams(1), name="ssd_proj")(
            x, g.reshape(1, d), w_z, w_x, w_t2, w_tt, conv_w, conv_b.reshape(1, conv_dim), cs0)


def _ssd_seq_kernel(zs_ref, xc_ref, dtc_ref, dtr_ref, dtb_c_ref, dtb_r_ref, al_c_ref, al_r_ref, dsk_ref, nw_ref,
                    s0_ref, y_ref, so_ref, st_ref, yb_ref, wide_ref, *, heads, head_p, groups, d_state):
    rows = zs_ref.shape[0]
    d_inner = heads * head_p
    gw = (heads // groups) * head_p
    ppg = gw // LANES
    hpt = LANES // head_p

    @pl.when(pl.program_id(1) == 0)
    def _():
        for g in range(groups):
            st_ref[g] = s0_ref[0, g].T

    r_i = lax.broadcasted_iota(jnp.int32, (rows, rows), 0)
    c_i = lax.broadcasted_iota(jnp.int32, (rows, rows), 1)
    causal = c_i <= r_i
    m_tri = jnp.where(causal, 1.0, 0.0).astype(BF16)
    neg_inf = jnp.float32(-jnp.inf)

    dt_c = jax.nn.softplus(dtc_ref[...] + dtb_c_ref[...])
    dt_r = jax.nn.softplus(dtr_ref[0] + dtb_r_ref[...])
    da_c = dt_c * ((-1.0 / LN2) * jnp.exp(al_c_ref[...]))
    da_r = dt_r * ((-1.0 / LN2) * jnp.exp(al_r_ref[...]))
    hi, lo = _split2(da_c)
    cum_c = _dot(m_tri, lo) + _dot(m_tri, hi)
    hi, lo = _split2(da_r)
    cum_r = _dot_nt(lo, m_tri) + _dot_nt(hi, m_tri)
    tot_c = cum_c[rows - 1:rows, :]
    stack = jnp.concatenate([dt_c * jnp.exp2(tot_c - cum_c), jnp.exp2(cum_c),
                             jnp.broadcast_to(jnp.exp2(tot_c), (8, 2 * heads))], axis=0)
    hi, lo = _split2(stack)
    k_i = lax.broadcasted_iota(jnp.int32, stack.shape, 1)
    e_r = lax.broadcasted_iota(jnp.int32, (2 * heads, d_inner), 0)
    e_c = lax.broadcasted_iota(jnp.int32, (2 * heads, d_inner), 1)
    expand = jnp.where((e_r % heads) == (e_c // head_p), 1.0, 0.0).astype(BF16)
    wide_ref[...] = _dot(jnp.where(k_i < heads, hi, lo), expand)
    low = lax.broadcasted_iota(jnp.int32, (rows, LANES), 1) < head_p

    def group(g):
        b32 = xc_ref[:, d_inner + g * d_state:d_inner + (g + 1) * d_state]
        b16 = b32.astype(BF16)
        c16 = xc_ref[:, d_inner + (groups + g) * d_state:d_inner + (groups + g + 1) * d_state].astype(BF16)
        cb = _dot_nt(c16, b16)
        bt16 = b32.T.astype(BF16)
        gs = slice(g * gw, (g + 1) * gw)
        st_old = st_ref[g]
        y_inter = _dot(c16, st_old.astype(BF16))
        yield
        xe_parts = []
        for p in range(ppg):
            pg = g * ppg + p
            ps = slice(pg * LANES, (pg + 1) * LANES)
            x2 = xc_ref[:, ps]
            gm = []
            for i in range(hpt):
                h = pg * hpt + i
                diff = cum_c[:, h:h + 1] - cum_r[h:h + 1, :]
                gm.append((cb * jnp.exp2(jnp.where(causal, diff, neg_inf)) * dt_r[h:h + 1, :]).astype(BF16))
            xm = [jnp.where(low if i == 0 else ~low, x2, 0.0).astype(BF16) for i in range(hpt)]
            if rows % LANES == 0:
                y2 = _dot(jnp.concatenate(gm, axis=1), jnp.concatenate(xm, axis=0))
            else:
                y2 = _dot(gm[0], xm[0]) + _dot(gm[1], xm[1])
            yb_ref[:, ps] = y2 + dsk_ref[:, ps] * x2
            xe_parts.append((x2 * wide_ref[0:rows, ps]).astype(BF16))
        yield
        xe = xe_parts[0] if ppg == 1 else jnp.concatenate(xe_parts, axis=1)
        st_ref[g] = st_old * wide_ref[2 * rows:2 * rows + 1, gs] + _dot(bt16, xe)
        yg = (yb_ref[:, gs] + y_inter * wide_ref[rows:2 * rows, gs]) * zs_ref[:, gs]
        yield
        ms = jnp.mean(yg * yg, axis=-1, keepdims=True)
        y_ref[:, gs] = (yg * lax.rsqrt(ms + EPS) * nw_ref[:, gs]).astype(y_ref.dtype)

    _lock_step([group(g) for g in range(groups)])

    @pl.when(pl.program_id(1) == pl.num_programs(1) - 1)
    def _():
        for g in range(groups):
            so_ref[0, g] = st_ref[g].T


def _ssd_seq_scan(zs, xc, dt_c, dt_r, dt_bias, a_log, d_skip, norm_w, s0, batch, t, t_blk):
    n, d_inner = zs.shape
    conv_dim = xc.shape[1]
    heads = dt_r.shape[1]
    head_p = d_inner // heads
    groups, gw, d_state = s0.shape[1:]
    assert gw * groups == d_inner and d_state == LANES and LANES // head_p == 2 and gw % LANES == 0
    nj = t // t_blk
    bmap4 = (lambda i, j: (i, 0, 0, 0)) if s0.shape[0] == batch else (lambda i, j: (0, 0, 0, 0))
    rowmap = lambda i, j: (i * nj + j, 0)
    s_block = (1, groups, gw, d_state)
    dup = lambda a: jnp.concatenate([a, a]).reshape(1, 2 * heads)
    return pl.pallas_call(
        functools.partial(_ssd_seq_kernel, heads=heads, head_p=head_p, groups=groups, d_state=d_state),
        grid=(batch, nj),
        in_specs=[pl.BlockSpec((t_blk, d_inner), rowmap), pl.BlockSpec((t_blk, conv_dim), rowmap),
                  pl.BlockSpec((t_blk, 2 * heads), rowmap),
                  pl.BlockSpec((1, heads, t_blk), lambda i, j: (i * nj + j, 0, 0)),
                  _const_spec((1, 2 * heads)), _const_spec((heads, 1)), _const_spec((1, 2 * heads)),
                  _const_spec((heads, 1)), _const_spec((1, d_inner)), _const_spec((1, d_inner)),
                  pl.BlockSpec(s_block, bmap4)],
        out_specs=[pl.BlockSpec((t_blk, d_inner), rowmap), pl.BlockSpec(s_block, lambda i, j: (i, 0, 0, 0))],
        out_shape=[jax.ShapeDtypeStruct((n, d_inner), BF16),
                   jax.ShapeDtypeStruct((batch, groups, gw, d_state), F32)],
        scratch_shapes=[pltpu.VMEM((groups, d_state, gw), F32), pltpu.VMEM((t_blk, d_inner), F32),
                        pltpu.VMEM((2 * t_blk + 8, d_inner), F32)],
        compiler_params=_cparams(2), name="ssd_seq_scan")(
            zs, xc, dt_c, dt_r, dup(dt_bias), dt_bias.reshape(heads, 1), dup(a_log), a_log.reshape(heads, 1),
            jnp.repeat(d_skip, head_p).reshape(1, d_inner), norm_w.reshape(1, d_inner), s0)


def _post_ffn_kernel(*refs, n_mix, ff_chunk):
    x_ref = refs[0]
    o_refs = refs[1:1 + n_mix]
    wo_refs = refs[1 + n_mix:1 + 2 * n_mix]
    g_post, g_pre, wg_ref, wu_ref, wd_ref, g_fpost, y_ref = refs[1 + 2 * n_mix:]
    mix = _dot(o_refs[0][...], wo_refs[0][...])
    for o_ref, wo_ref in zip(o_refs[1:], wo_refs[1:]):
        mix = mix + _dot(o_ref[...], wo_ref[...])
    x1 = x_ref[...] + _rms(mix, g_post[...])
    h = _rms(x1, g_pre[...]).astype(BF16)
    d_ff = wg_ref.shape[1]
    ff = None
    for a in range(0, d_ff, ff_chunk):
        cs = slice(a, a + ff_chunk)
        act = (_silu(_dot(h, wg_ref[:, cs])) * _dot(h, wu_ref[:, cs])).astype(BF16)
        part = _dot(act, wd_ref[cs, :])
        ff = part if ff is None else ff + part
    y_ref[...] = x1 + _rms(ff, g_fpost[...])


def _post_ffn(x, mixes, w_outs, g_post, g_pre, w_gate, w_up, w_down, g_fpost, tile=512, ff_chunk=256):
    n, d = x.shape
    tm = min(tile, n)
    assert n % tm == 0 and w_gate.shape[1] % ff_chunk == 0
    row = lambda w: pl.BlockSpec((tm, w), lambda i: (i, 0))
    in_specs = ([row(d)] + [row(o.shape[1]) for o in mixes] + [_const_spec(w.shape) for w in w_outs]
                + [_const_spec((1, d)), _const_spec((1, d)), _const_spec(w_gate.shape), _const_spec(w_up.shape),
                   _const_spec(w_down.shape), _const_spec((1, d))])
    return pl.pallas_call(
        functools.partial(_post_ffn_kernel, n_mix=len(mixes), ff_chunk=ff_chunk),
        grid=(n // tm,), in_specs=in_specs, out_specs=row(d),
        out_shape=jax.ShapeDtypeStruct((n, d), F32),
        compiler_params=_cparams(1), name="post_ffn")(
            x, *mixes, *w_outs, g_post.reshape(1, d), g_pre.reshape(1, d), w_gate, w_up, w_down,
            g_fpost.reshape(1, d))


def _trunk(x, st, w):
    b, t, d = x.shape
    n = b * t
    xf = x.reshape(n, d)
    s_hgrn, s_gla, s_ssm, s_conv = st
    h_a, dk_a, dv_a = s_hgrn.shape[1:]
    h_b, dk_b, dv_b = s_gla.shape[1:]
    h_c, head_p, d_state = s_ssm.shape[1:]
    assert dk_a == LANES and dv_a == LANES and dk_b == HALF and dv_b == LANES and h_b % 2 == 0
    wa = h_a * LANES
    n_main0 = 4 * wa + 2 * h_b * dk_b + 2 * h_b * dv_b
    d_inner = h_c * head_p
    conv_dim = s_conv.shape[2]
    groups = (conv_dim - d_inner) // (2 * d_state)

    pm, alow = _norm_proj(xf, w["norm_mix_pre"][0], [w["ev_w_in_main"]], w_tail=w["ev_w_in_tail"])
    o_a, new_hgrn = _hgrn_scan(pm, w["hgrn_gamma"], w["ev_norm_a"], s_hgrn, 0, b, t, h_a)
    o_b, new_gla = _gla_scan(pm, alow, w["ev_w_alpha_up"], w["ev_b_alpha"], w["ev_norm_b"],
                             s_gla.reshape(s_gla.shape[0], h_b // 2, LANES, dv_b), b, t, h_b, 4 * wa)
    new_gla = new_gla.reshape(b, h_b, dk_b, dv_b)
    xf = _post_ffn(xf, [o_a, o_b], [w["ev_w_out_a"], w["ev_w_out_b"]], w["norm_mix_post"][0], w["norm_ffn_pre"][0],
                   w["ffn_w_gate"][0], w["ffn_w_up"][0], w["ffn_w_down"][0], w["norm_ffn_post"][0])

    nseq, t_blk, _ = _seq_blocking(b, t, LANES, LANES)
    s_ssm_g = s_ssm.reshape(s_ssm.shape[0], groups, (h_c // groups) * head_p, d_state)
    if nseq == 1:
        zs, xc, dt_c, dt_r, new_conv = _ssd_proj(
            xf, w["norm_mix_pre"][1], w["od_w_in_z"], w["od_w_in_x"], w["od_w_in_tail2"], w["od_w_in_tail_t"],
            w["od_conv_w"], w["od_conv_b"], s_conv, b, t, t_blk)
        y, new_ssm = _ssd_seq_scan(zs, xc, dt_c, dt_r, w["od_dt_bias"], w["od_a_log"], w["od_d_skip"],
                                   w["od_norm"], s_ssm_g, b, t, t_blk)
    else:
        z, xbc, dt_c, dt_r = _norm_proj(xf, w["norm_mix_pre"][1], [w["od_w_in_z"], w["od_w_in_x"]],
                                        w_tail=w["od_w_in_tail"], w_tail_t=w["od_w_in_tail_t"],
                                        tail_t_rows=nseq * t_blk)
        y, new_conv, new_ssm = _ssd_scan(z, xbc, dt_c, dt_r, w["od_conv_w"], w["od_conv_b"], w["od_dt_bias"],
                                         w["od_a_log"], w["od_d_skip"], w["od_norm"], s_conv, s_ssm_g,
                                         b, t, nseq, t_blk)
    new_ssm = new_ssm.reshape(b, h_c, head_p, d_state)
    xf = _post_ffn(xf, [y], [w["od_w_out"]], w["norm_mix_post"][1], w["norm_ffn_pre"][1],
                   w["ffn_w_gate"][1], w["ffn_w_up"][1], w["ffn_w_down"][1], w["norm_ffn_post"][1])
    return xf.reshape(b, t, d), (new_hgrn, new_gla, new_ssm, new_conv)


def kernel(x_prompt, x_sample, state_hgrn, state_gla, state_ssm, state_conv, meta_tokens, hgrn_gamma, norm_mix_pre, norm_mix_post, norm_ffn_pre, norm_ffn_post, ev_w_in, ev_w_alpha_up, ev_b_alpha, ev_norm_a, ev_norm_b, ev_w_out, od_w_in, od_conv_w, od_conv_b, od_dt_bias, od_a_log, od_d_skip, od_norm, od_w_out, ffn_w_gate, ffn_w_up, ffn_w_down):
    n_even, n_odd = state_hgrn.shape[0], state_ssm.shape[0]
    assert n_even == 1 and n_odd == 1 and norm_mix_pre.shape[0] == 2
    h_a, dk_a, dv_a = state_hgrn.shape[2:]
    h_b, dk_b, dv_b = state_gla.shape[2:]
    h_c, head_p, d_state = state_ssm.shape[2:]
    d_inner = h_c * head_p
    conv_dim = state_conv.shape[3]
    rank = ev_w_alpha_up.shape[1]
    n_main0 = ev_w_in.shape[2] - rank
    n_main1 = d_inner + conv_dim
    cast = lambda a: a.astype(BF16)
    w = {
        "hgrn_gamma": hgrn_gamma.astype(F32),
        "norm_mix_pre": norm_mix_pre, "norm_mix_post": norm_mix_post,
        "norm_ffn_pre": norm_ffn_pre, "norm_ffn_post": norm_ffn_post,
        "ev_w_in_main": cast(ev_w_in[0, :, :n_main0]), "ev_w_in_tail": cast(ev_w_in[0, :, n_main0:]),
        "ev_w_alpha_up": ev_w_alpha_up[0].astype(BF16), "ev_b_alpha": ev_b_alpha[0],
        "ev_norm_a": ev_norm_a[0], "ev_norm_b": ev_norm_b[0],
        "ev_w_out_a": cast(ev_w_out[0, :h_a * dv_a]), "ev_w_out_b": cast(ev_w_out[0, h_a * dv_a:]),
        "od_w_in_z": cast(od_w_in[0, :, :d_inner]), "od_w_in_x": cast(od_w_in[0, :, d_inner:n_main1]),
        "od_w_in_tail": cast(od_w_in[0, :, n_main1:]),
        "od_w_in_tail2": cast(jnp.concatenate([od_w_in[0, :, n_main1:]] * 2, axis=1)),
        "od_w_in_tail_t": cast(od_w_in[0, :, n_main1:].T),
        "od_conv_w": od_conv_w[0], "od_conv_b": od_conv_b[0], "od_dt_bias": od_dt_bias[0],
        "od_a_log": od_a_log[0], "od_d_skip": od_d_skip[0], "od_norm": od_norm[0],
        "od_w_out": od_w_out[0].astype(BF16),
        "ffn_w_gate": ffn_w_gate.astype(BF16), "ffn_w_up": ffn_w_up.astype(BF16),
        "ffn_w_down": ffn_w_down.astype(BF16),
    }
    bp = x_prompt.shape[0]
    n_meta = meta_tokens.shape[0]
    zeros = (jnp.zeros((1, h_a, dk_a, dv_a), F32), jnp.zeros((1, h_b, dk_b, dv_b), F32),
             jnp.zeros((1, h_c, head_p, d_state), F32), jnp.zeros((1, state_conv.shape[2], conv_dim), F32))
    _, st_meta = _trunk(meta_tokens.astype(x_prompt.dtype)[None], zeros, w)
    yp, st_p = _trunk(x_prompt, st_meta, w)
    ys, st_s = _trunk(x_sample, (state_hgrn[0], state_gla[0], state_ssm[0], state_conv[0]), w)
    del bp, n_meta
    return (yp, ys, st_p[0][None], st_p[1][None], st_p[2][None], st_p[3][None],
            st_s[0][None], st_s[1][None], st_s[2][None], st_s[3][None])
```
